```python
import jax, jax.numpy as jnp
from jax import lax
import numpy as np

D_MODEL = 1024
BATCH = 8
SEQ = 4096
DEPTH = 4

N_ATTN_HEADS = 8
HEAD_DIM = 64
ATTN_WIDTH = N_ATTN_HEADS * HEAD_DIM
DILATED_PATTERNS = ((128, 1), (512, 4), (2048, 16))
SPAN = 128
MAX_DILATION = 16
PAD_MULT = MAX_DILATION * SPAN
LRU_WIDTH = D_MODEL // 2
N_LRU_BLOCKS = 8
LRU_BLOCK = LRU_WIDTH // N_LRU_BLOCKS
CONV_WIDTH = 4
LRU_C = 8.0
MIX_WIDTH = ATTN_WIDTH + LRU_WIDTH
IN_WIDTH = 3 * ATTN_WIDTH + 2 * LRU_WIDTH
D_FF = 4 * D_MODEL
NORM_EPS = 1e-6

kernel_name = "hybrid_dilated_attn_rglru_block"


def rms_norm(x, g):
    xf = x.astype(jnp.float32)
    y = xf * lax.rsqrt(jnp.mean(xf * xf, axis=-1, keepdims=True) + NORM_EPS)
    return (y * g.astype(jnp.float32)).astype(x.dtype)


def dilated_branch(q, k, v, window, dilation):
    b, sp, h, dh = q.shape
    span = window // dilation
    nb = sp // (dilation * span)

    def split(t):
        return t.reshape(b, nb, span, dilation, h, dh).transpose(0, 3, 4, 1, 2, 5)

    def with_prev(t):
        prev = jnp.pad(t, ((0, 0), (0, 0), (0, 0), (1, 0), (0, 0), (0, 0)))[:, :, :, :-1]
        return jnp.concatenate([prev, t], axis=4)

    qb = split(q)
    kb = with_prev(split(k))
    vb = with_prev(split(v))
    s = jnp.einsum('bchnqd,bchnkd->bchnqk', qb, kb,
                   preferred_element_type=jnp.float32) * (HEAD_DIM ** -0.5)
    qi = jnp.arange(span)[:, None]
    kj = jnp.arange(2 * span)[None, :]
    dist = span + qi - kj
    blk = jnp.arange(nb)[:, None, None]
    valid = (dist >= 0) & (dist <= span) & ((blk > 0) | (kj >= span))
    s = jnp.where(valid, s, -jnp.inf)
    mx = jnp.max(s, axis=-1, keepdims=True)
    p = jnp.exp(s - mx)
    den = jnp.sum(p, axis=-1)
    o = jnp.einsum('bchnqk,bchnkd->bchnqd', p, vb.astype(jnp.float32)) / den[..., None]
    lse = mx[..., 0] + jnp.log(den)
    o = o.transpose(0, 3, 4, 1, 2, 5).reshape(b, sp, h, dh)
    lse = lse.transpose(0, 3, 4, 1, 2).reshape(b, sp, h)
    return o, lse


def dilated_attention(q, k, v):
    b, s, h, dh = q.shape
    sp = -(-s // PAD_MULT) * PAD_MULT
    pad = ((0, 0), (0, sp - s), (0, 0), (0, 0))
    qp, kp, vp = jnp.pad(q, pad), jnp.pad(k, pad), jnp.pad(v, pad)
    outs, lses = [], []
    for window, dilation in DILATED_PATTERNS:
        o, l = dilated_branch(qp, kp, vp, window, dilation)
        outs.append(o)
        lses.append(l)
    wts = jax.nn.softmax(jnp.stack(lses, axis=0), axis=0)
    o = jnp.einsum('pbsh,pbshd->bshd', wts, jnp.stack(outs, axis=0))
    return o[:, :s].astype(q.dtype)


def causal_depthwise_conv(x, w, bias):
    c = x.shape[-1]
    y = lax.conv_general_dilated(x, w[:, None, :].astype(x.dtype), window_strides=(1,),
                                 padding=[(CONV_WIDTH - 1, 0)],
                                 dimension_numbers=('NWC', 'WIO', 'NWC'),
                                 feature_group_count=c)
    return y + bias.astype(x.dtype)


def rg_lru(x, w_r, b_r, w_i, b_i, lam):
    b, s, c = x.shape
    xf = x.astype(jnp.float32)
    xb = xf.reshape(b, s, N_LRU_BLOCKS, LRU_BLOCK)
    r = jax.nn.sigmoid(jnp.einsum('bsnc,ncd->bsnd', xb, w_r.astype(jnp.float32)).reshape(b, s, c)
                       + b_r.astype(jnp.float32))
    i = jax.nn.sigmoid(jnp.einsum('bsnc,ncd->bsnd', xb, w_i.astype(jnp.float32)).reshape(b, s, c)
                       + b_i.astype(jnp.float32))
    log_a = -LRU_C * r * jax.nn.softplus(-lam.astype(jnp.float32))
    a = jnp.exp(log_a)
    u = jnp.sqrt(-jnp.expm1(2.0 * log_a)) * (i * xf)

    def combine(left, right):
        a_l, h_l = left
        a_r, h_r = right
        return a_l * a_r, a_r * h_l + h_r

    _, h = lax.associative_scan(combine, (a, u), axis=1)
    return h.astype(x.dtype)


def hybrid_mixer(h, w_in, conv_w, conv_b, w_r, b_r, w_i, b_i, lam, w_out):
    b, s, _ = h.shape
    z = h @ w_in
    q, k, v, xr, gr = jnp.split(
        z, [ATTN_WIDTH, 2 * ATTN_WIDTH, 3 * ATTN_WIDTH, 3 * ATTN_WIDTH + LRU_WIDTH], axis=-1)
    shp = (b, s, N_ATTN_HEADS, HEAD_DIM)
    attn = dilated_attention(q.reshape(shp), k.reshape(shp), v.reshape(shp)).reshape(b, s, ATTN_WIDTH)
    xr = causal_depthwise_conv(xr, conv_w, conv_b)
    rec = rg_lru(xr, w_r, b_r, w_i, b_i, lam) * jax.nn.gelu(gr)
    return jnp.concatenate([attn.astype(h.dtype), rec.astype(h.dtype)], axis=-1) @ w_out


def squared_relu_mlp(h, w1, w2):
    a = jax.nn.relu(h @ w1)
    return (a * a) @ w2


def setup_inputs(seed: int = 0) -> dict:
    key = jax.random.key(seed)
    ks = jax.random.split(key, 20)
    f32 = jnp.float32

    def gain(k):
        return 1.0 + 0.05 * jax.random.normal(k, (DEPTH, D_MODEL), f32)

    u = jax.random.uniform(ks[10], (DEPTH, LRU_WIDTH), f32, 0.9, 0.999)
    a0 = u ** (1.0 / LRU_C)
    lru_lambda = jnp.log(a0) - jnp.log1p(-a0)
    return {
        "x": jax.random.normal(ks[0], (BATCH, SEQ, D_MODEL), f32),
        "mix_norm_pre": gain(ks[1]),
        "mix_norm_post": gain(ks[2]),
        "mlp_norm_pre": gain(ks[3]),
        "mlp_norm_post": gain(ks[4]),
        "w_in": jax.random.normal(ks[5], (DEPTH, D_MODEL, IN_WIDTH), f32) * D_MODEL ** -0.5,
        "conv_w": jax.random.normal(ks[6], (DEPTH, CONV_WIDTH, LRU_WIDTH), f32) * CONV_WIDTH ** -0.5,
        "conv_b": 0.01 * jax.random.normal(ks[7], (DEPTH, LRU_WIDTH), f32),
        "w_rgate": jax.random.normal(ks[8], (DEPTH, N_LRU_BLOCKS, LRU_BLOCK, LRU_BLOCK), f32) * LRU_BLOCK ** -0.5,
        "b_rgate": 0.01 * jax.random.normal(ks[9], (DEPTH, LRU_WIDTH), f32),
        "w_igate": jax.random.normal(ks[11], (DEPTH, N_LRU_BLOCKS, LRU_BLOCK, LRU_BLOCK), f32) * LRU_BLOCK ** -0.5,
        "b_igate": 0.01 * jax.random.normal(ks[12], (DEPTH, LRU_WIDTH), f32),
        "lru_lambda": lru_lambda,
        "w_out": jax.random.normal(ks[13], (DEPTH, MIX_WIDTH, D_MODEL), f32) * MIX_WIDTH ** -0.5,
        "w_ff_in": jax.random.normal(ks[14], (DEPTH, D_MODEL, D_FF), f32) * D_MODEL ** -0.5,
        "w_ff_out": jax.random.normal(ks[15], (DEPTH, D_FF, D_MODEL), f32) * D_FF ** -0.5,
    }


def reference(x, mix_norm_pre, mix_norm_post, mlp_norm_pre, mlp_norm_post, w_in, conv_w, conv_b,
              w_rgate, b_rgate, w_igate, b_igate, lru_lambda, w_out, w_ff_in, w_ff_out):
    for l in range(DEPTH):
        h = rms_norm(x, mix_norm_pre[l])
        m = hybrid_mixer(h, w_in[l], conv_w[l], conv_b[l], w_rgate[l], b_rgate[l],
                         w_igate[l], b_igate[l], lru_lambda[l], w_out[l])
        x = x + rms_norm(m, mix_norm_post[l])
        h = rms_norm(x, mlp_norm_pre[l])
        x = x + rms_norm(squared_relu_mlp(h, w_ff_in[l], w_ff_out[l]), mlp_norm_post[l])
    return x
```

```python
import functools

import jax
import jax.numpy as jnp
from jax import lax
from jax.experimental import pallas as pl
from jax.experimental.pallas import tpu as pltpu

D_MODEL = 1024
N_HEADS = 8
HEAD_DIM = 64
ATTN_WIDTH = N_HEADS * HEAD_DIM
DILATIONS = (1, 4, 16)
SPAN = 128
LRU_WIDTH = 512
N_LRU_BLOCKS = 8
LRU_BLOCK = LRU_WIDTH // N_LRU_BLOCKS
CONV_WIDTH = 4
LRU_C = 8.0
IN_WIDTH = 3 * ATTN_WIDTH + 2 * LRU_WIDTH
D_FF = 4 * D_MODEL
NORM_EPS = 1e-6

LANES = 128
SUBLANES = 8
MXU_DIM = 256
VMEM_LIMIT = 56 * 1024 * 1024

ROW_TILE = 512
N_CHUNK = 512
LRU_TIME_CHUNK = 256
HEAD_PAIRS = ATTN_WIDTH // LANES
LRU_GROUPS = LRU_WIDTH // LANES
XR_COL_BLOCK = 3 * ATTN_WIDTH // LRU_WIDTH
GR_COL_BLOCK = XR_COL_BLOCK + 1

bf16 = jnp.bfloat16
f32 = jnp.float32


def _rms(x, g):
    y = x * lax.rsqrt(jnp.mean(x * x, axis=-1, keepdims=True) + NORM_EPS)
    return y * g


def _in_proj_kernel(x_ref, g_ref, w_ref, z_ref):
    h = _rms(x_ref[...], g_ref[...]).astype(bf16)
    for n in range(IN_WIDTH // N_CHUNK):
        cols = slice(n * N_CHUNK, (n + 1) * N_CHUNK)
        z_ref[:, cols] = jnp.dot(h, w_ref[:, cols], preferred_element_type=f32)


def _in_proj(x2d, g, w_bf16):
    t = x2d.shape[0]
    return pl.pallas_call(
        _in_proj_kernel,
        grid=(t // ROW_TILE,),
        in_specs=[
            pl.BlockSpec((ROW_TILE, D_MODEL), lambda i: (i, 0)),
            pl.BlockSpec((1, D_MODEL), lambda i: (0, 0)),
            pl.BlockSpec((D_MODEL, IN_WIDTH), lambda i: (0, 0), pipeline_mode=pl.Buffered(1)),
        ],
        out_specs=pl.BlockSpec((ROW_TILE, IN_WIDTH), lambda i: (i, 0)),
        out_shape=jax.ShapeDtypeStruct((t, IN_WIDTH), f32),
        compiler_params=pltpu.CompilerParams(
            dimension_semantics=("parallel",), vmem_limit_bytes=VMEM_LIMIT),
        name="in_proj",
    )(x2d, g, w_bf16)


def _attn_kernel(q_ref, k_ref, v_ref, o_ref, ob_ref, lb_ref, *, seq):
    lane = lax.broadcasted_iota(jnp.int32, (1, LANES), 1)
    head0 = lane < HEAD_DIM
    qi = lax.broadcasted_iota(jnp.int32, (SPAN, 2 * SPAN), 0)
    kj = lax.broadcasted_iota(jnp.int32, (SPAN, 2 * SPAN), 1)
    dist = SPAN + qi - kj
    band = (dist >= 0) & (dist <= SPAN)
    causal = (lax.broadcasted_iota(jnp.int32, (SPAN, SPAN), 0)
              >= lax.broadcasted_iota(jnp.int32, (SPAN, SPAN), 1))
    nt = (((1,), (1,)), ((), ()))

    def unit(p, d, qstart, first):
        rows = pl.ds(qstart, SPAN, stride=d) if d > 1 else pl.ds(qstart, SPAN)
        if first:
            krows = rows
            mask = causal
        else:
            kstart = qstart - d * SPAN
            krows = (pl.ds(kstart, 2 * SPAN, stride=d) if d > 1
                     else pl.ds(kstart, 2 * SPAN))
            mask = band
        q = (q_ref[0, rows, :] * (HEAD_DIM ** -0.5)).astype(bf16)
        k = k_ref[0, krows, :]
        v = v_ref[0, krows, :]
        o_pair = None
        lse = []
        for hsel in (head0, ~head0):
            km = jnp.where(hsel, k, 0.0).astype(bf16)
            vm = jnp.where(hsel, v, 0.0).astype(bf16)
            s = lax.dot_general(q, km, nt, preferred_element_type=f32)
            s = jnp.where(mask, s, -jnp.inf)
            mx = jnp.max(s, axis=-1, keepdims=True)
            e = jnp.exp(s - mx)
            den = jnp.sum(e, axis=-1, keepdims=True)
            pv = jnp.dot(e.astype(bf16), vm, preferred_element_type=f32)
            o_h = pv / den
            o_pair = o_h if o_pair is None else o_pair + o_h
            lse.append(mx + jnp.log(den))
        ob_ref[p, rows, :] = o_pair
        lb_ref[p, rows, :] = jnp.where(head0, lse[0], lse[1])

    for p, d in enumerate(DILATIONS):
        nb = seq // (d * SPAN)

        def per_class(c, carry, p=p, d=d, nb=nb):
            unit(p, d, c, True)

            def per_block(n, carry2):
                unit(p, d, c + n * (d * SPAN), False)
                return carry2

            lax.fori_loop(1, nb, per_block, 0)
            return carry

        lax.fori_loop(0, d, per_class, 0)

    merge_rows = 256

    def merge(i, carry):
        rows = pl.ds(pl.multiple_of(i * merge_rows, merge_rows), merge_rows)
        l0, l1, l2 = lb_ref[0, rows, :], lb_ref[1, rows, :], lb_ref[2, rows, :]
        mx = jnp.maximum(jnp.maximum(l0, l1), l2)
        e0, e1, e2 = jnp.exp(l0 - mx), jnp.exp(l1 - mx), jnp.exp(l2 - mx)
        den = e0 + e1 + e2
        o = (e0 / den) * ob_ref[0, rows, :] + (e1 / den) * ob_ref[1, rows, :] \
            + (e2 / den) * ob_ref[2, rows, :]
        o_ref[0, rows, :] = o
        return carry

    lax.fori_loop(0, seq // merge_rows, merge, 0)


def _attention(z3):
    b, s, _ = z3.shape
    blk = (1, s, LANES)
    return pl.pallas_call(
        functools.partial(_attn_kernel, seq=s),
        grid=(b, HEAD_PAIRS),
        in_specs=[
            pl.BlockSpec(blk, lambda i, j: (i, 0, j)),
            pl.BlockSpec(blk, lambda i, j: (i, 0, HEAD_PAIRS + j)),
            pl.BlockSpec(blk, lambda i, j: (i, 0, 2 * HEAD_PAIRS + j)),
        ],
        out_specs=pl.BlockSpec(blk, lambda i, j: (i, 0, j)),
        out_shape=jax.ShapeDtypeStruct((b, s, ATTN_WIDTH), f32),
        scratch_shapes=[
            pltpu.VMEM((len(DILATIONS), s, LANES), f32),
            pltpu.VMEM((len(DILATIONS), s, LANES), f32),
        ],
        compiler_params=pltpu.CompilerParams(
            dimension_semantics=("parallel", "parallel"), vmem_limit_bytes=VMEM_LIMIT),
        name="dilated_attention",
    )(z3, z3, z3)


def _rglru_kernel(xr_ref, xprev_ref, gr_ref, cw_ref, cb_ref, wg_ref, br_ref, bi_ref, lam_ref,
                  o_ref, xe_ref, a_ref, u_ref, h_ref, *, batch):
    tc = LRU_TIME_CHUNK
    step = pl.program_id(0)

    @pl.when(step == 0)
    def _():
        h_ref[...] = jnp.zeros_like(h_ref)

    neg_lam = -lam_ref[...]
    softplus = jnp.maximum(neg_lam, 0.0) + jnp.log1p(jnp.exp(-jnp.abs(neg_lam)))
    rate = -LRU_C * softplus
    half = LRU_WIDTH // 2

    def dense(b, carry):
        prev = xprev_ref[b]
        xe_ref[0:SUBLANES, :] = jnp.where(step == 0, 0.0, prev)
        xe_ref[SUBLANES:, :] = xr_ref[b]
        y = cb_ref[...] + cw_ref[CONV_WIDTH - 1:CONV_WIDTH, :] * xe_ref[SUBLANES:, :]
        for j in range(1, CONV_WIDTH):
            tap = CONV_WIDTH - 1 - j
            y = y + cw_ref[tap:tap + 1, :] * xe_ref[pl.ds(SUBLANES - j, tc), :]
        yb = y.astype(bf16)
        gates = [jnp.dot(yb[:, hf * half:(hf + 1) * half], wg_ref[hf],
                         preferred_element_type=f32) for hf in range(2)]
        pre_r = jnp.concatenate([gates[0][:, :half], gates[1][:, :half]], axis=-1)
        pre_i = jnp.concatenate([gates[0][:, half:], gates[1][:, half:]], axis=-1)
        r = jax.nn.sigmoid(pre_r + br_ref[...])
        i = jax.nn.sigmoid(pre_i + bi_ref[...])
        log_a = rate * r
        a = jnp.exp(log_a)
        u = jnp.sqrt(-jnp.tanh(log_a) * (a * a + 1.0)) * (i * y)
        for g in range(LRU_GROUPS):
            cols = slice(g * LANES, (g + 1) * LANES)
            a_ref[g, pl.ds(b, tc, stride=batch), :] = a[:, cols]
            u_ref[g, pl.ds(b, tc, stride=batch), :] = u[:, cols]
        return carry

    lax.fori_loop(0, batch, dense, 0)

    def scan(t, hs):
        rows = pl.ds(pl.multiple_of(t * batch, batch), batch)
        new = []
        for g in range(LRU_GROUPS):
            h = a_ref[g, rows, :] * hs[g] + u_ref[g, rows, :]
            u_ref[g, rows, :] = h
            new.append(h)
        return tuple(new)

    hs = lax.fori_loop(0, tc, scan, tuple(h_ref[g] for g in range(LRU_GROUPS)), unroll=8)
    for g in range(LRU_GROUPS):
        h_ref[g] = hs[g]

    def emit(b, carry):
        h = jnp.concatenate(
            [u_ref[g, pl.ds(b, tc, stride=batch), :] for g in range(LRU_GROUPS)], axis=-1)
        o_ref[b] = h * jax.nn.gelu(gr_ref[b])
        return carry

    lax.fori_loop(0, batch, emit, 0)


def _rglru(z3, conv_w, conv_b, w_gates, b_r, b_i, lam):
    b, s, _ = z3.shape
    assert b == SUBLANES
    tc = LRU_TIME_CHUNK
    blk = (b, tc, LRU_WIDTH)
    vec = pl.BlockSpec((1, LRU_WIDTH), lambda i: (0, 0))
    return pl.pallas_call(
        functools.partial(_rglru_kernel, batch=b),
        grid=(s // tc,),
        in_specs=[
            pl.BlockSpec(blk, lambda i: (0, i, XR_COL_BLOCK)),
            pl.BlockSpec((b, SUBLANES, LRU_WIDTH),
                         lambda i: (0, jnp.maximum(i * (tc // SUBLANES) - 1, 0), XR_COL_BLOCK)),
            pl.BlockSpec(blk, lambda i: (0, i, GR_COL_BLOCK)),
            pl.BlockSpec((CONV_WIDTH, LRU_WIDTH), lambda i: (0, 0)),
            vec,
            pl.BlockSpec((2, LRU_WIDTH // 2, LRU_WIDTH), lambda i: (0, 0, 0)),
            vec, vec, vec,
        ],
        out_specs=pl.BlockSpec(blk, lambda i: (0, i, 0)),
        out_shape=jax.ShapeDtypeStruct((b, s, LRU_WIDTH), f32),
        scratch_shapes=[
            pltpu.VMEM((tc + SUBLANES, LRU_WIDTH), f32),
            pltpu.VMEM((LRU_GROUPS, tc * b, LANES), f32),
            pltpu.VMEM((LRU_GROUPS, tc * b, LANES), f32),
            pltpu.VMEM((LRU_GROUPS, b, LANES), f32),
        ],
        compiler_params=pltpu.CompilerParams(
            dimension_semantics=("arbitrary",), vmem_limit_bytes=VMEM_LIMIT),
        name="rglru",
    )(z3, z3, z3, conv_w, conv_b, w_gates, b_r, b_i, lam)


def _gate_weights(w_r, w_i):
    def half_dense(w, hf):
        blocks = w[hf * 4:(hf + 1) * 4]
        return jax.scipy.linalg.block_diag(*[blocks[n] for n in range(4)])
    return jnp.stack([
        jnp.concatenate([half_dense(w_r, hf), half_dense(w_i, hf)], axis=-1) for hf in range(2)
    ]).astype(bf16)


def _out_mlp_kernel(attn_ref, rec_ref, x_ref, gpost_ref, gpre2_ref, gpost2_ref,
                    wo_ref, w1_ref, w2_ref, o_ref):
    mix = jnp.concatenate([attn_ref[...].astype(bf16), rec_ref[...].astype(bf16)], axis=-1)
    m = jnp.dot(mix, wo_ref[...], preferred_element_type=f32)
    x1 = x_ref[...] + _rms(m, gpost_ref[...])
    h = _rms(x1, gpre2_ref[...]).astype(bf16)
    y = None
    for n in range(D_FF // (2 * N_CHUNK)):
        cols = slice(n * 2 * N_CHUNK, (n + 1) * 2 * N_CHUNK)
        a = jnp.maximum(jnp.dot(h, w1_ref[:, cols], preferred_element_type=f32), 0.0)
        part = jnp.dot((a * a).astype(bf16), w2_ref[cols, :], preferred_element_type=f32)
        y = part if y is None else y + part
    o_ref[...] = x1 + _rms(y, gpost2_ref[...])


def _out_mlp(attn2d, rec2d, x2d, g_post, g_pre2, g_post2, wo, w1, w2):
    t = x2d.shape[0]
    vec = pl.BlockSpec((1, D_MODEL), lambda i: (0, 0))
    const = lambda shape: pl.BlockSpec(shape, lambda i: (0, 0), pipeline_mode=pl.Buffered(1))
    return pl.pallas_call(
        _out_mlp_kernel,
        grid=(t // ROW_TILE,),
        in_specs=[
            pl.BlockSpec((ROW_TILE, ATTN_WIDTH), lambda i: (i, 0)),
            pl.BlockSpec((ROW_TILE, LRU_WIDTH), lambda i: (i, 0)),
            pl.BlockSpec((ROW_TILE, D_MODEL), lambda i: (i, 0)),
            vec, vec, vec,
            const((ATTN_WIDTH + LRU_WIDTH, D_MODEL)),
            const((D_MODEL, D_FF)),
            const((D_FF, D_MODEL)),
        ],
        out_specs=pl.BlockSpec((ROW_TILE, D_MODEL), lambda i: (i, 0)),
        out_shape=jax.ShapeDtypeStruct((t, D_MODEL), f32),
        compiler_params=pltpu.CompilerParams(
            dimension_semantics=("parallel",), vmem_limit_bytes=VMEM_LIMIT),
        name="out_mlp",
    )(attn2d, rec2d, x2d, g_post, g_pre2, g_post2, wo, w1, w2)


def kernel(x, mix_norm_pre, mix_norm_post, mlp_norm_pre, mlp_norm_post, w_in, conv_w, conv_b,
           w_rgate, b_rgate, w_igate, b_igate, lru_lambda, w_out, w_ff_in, w_ff_out):
    b, s, d = x.shape
    depth = w_in.shape[0]
    t = b * s
    x2d = x.reshape(t, d)
    row = lambda v: v.reshape(1, -1)
    for l in range(depth):
        z = _in_proj(x2d, row(mix_norm_pre[l]), w_in[l].astype(bf16))
        z3 = z.reshape(b, s, IN_WIDTH)
        attn = _attention(z3)
        rec = _rglru(z3, conv_w[l], row(conv_b[l]), _gate_weights(w_rgate[l], w_igate[l]),
                     row(b_rgate[l]), row(b_igate[l]), row(lru_lambda[l]))
        x2d = _out_mlp(attn.reshape(t, ATTN_WIDTH), rec.reshape(t, LRU_WIDTH), x2d,
                       row(mix_norm_post[l]), row(mlp_norm_pre[l]), row(mlp_norm_post[l]),
                       w_out[l].astype(bf16), w_ff_in[l].astype(bf16), w_ff_out[l].astype(bf16))
    return x2d.reshape(b, s, d)
```

```python
import functools
import math

import jax
import jax.numpy as jnp
from jax import lax
from jax.experimental import pallas as pl
from jax.experimental.pallas import tpu as pltpu

D_MODEL = 1024
N_HEADS = 8
HEAD_DIM = 64
ATTN_WIDTH = N_HEADS * HEAD_DIM
DILATIONS = (1, 4, 16)
MAX_DILATION = 16
SPAN = 128
LRU_WIDTH = 512
N_LRU_BLOCKS = 8
LRU_BLOCK = LRU_WIDTH // N_LRU_BLOCKS
CONV_WIDTH = 4
LRU_C = 8.0
IN_WIDTH = 3 * ATTN_WIDTH + 2 * LRU_WIDTH
D_FF = 4 * D_MODEL
NORM_EPS = 1e-6

LANES = 128
SUBLANES = 8
VMEM_LIMIT = 56 * 1024 * 1024

ROW_TILE = 512
N_CHUNK = 512
LRU_TIME_CHUNK = 256
HEAD_PAIRS = ATTN_WIDTH // LANES
QKV_SLABS = 3 * HEAD_PAIRS
LRU_GROUPS = LRU_WIDTH // LANES
ATTN_GROUP = 8
Q_SCALE = HEAD_DIM ** -0.5 * math.log2(math.e)

bf16 = jnp.bfloat16
f32 = jnp.float32


def _rms(x, g):
    y = x * lax.rsqrt(jnp.mean(x * x, axis=-1, keepdims=True) + NORM_EPS)
    return y * g


def _in_proj_kernel(x_ref, g_ref, w_ref, qkv_ref, zr_ref, zs_ref):
    h = _rms(x_ref[...], g_ref[...]).astype(bf16)
    slabs_per_chunk = N_CHUNK // LANES
    for n in range(IN_WIDTH // N_CHUNK):
        z = jnp.dot(h, w_ref[:, n * N_CHUNK:(n + 1) * N_CHUNK], preferred_element_type=f32)
        if n < QKV_SLABS // slabs_per_chunk:
            for s in range(slabs_per_chunk):
                zs_ref[n * slabs_per_chunk + s] = z[:, s * LANES:(s + 1) * LANES]
        else:
            m = n - QKV_SLABS // slabs_per_chunk
            zr_ref[:, m * N_CHUNK:(m + 1) * N_CHUNK] = z
    rows = ROW_TILE // MAX_DILATION
    for slab in range(QKV_SLABS):
        for c in range(MAX_DILATION):
            qkv_ref[0, slab, c] = zs_ref[slab, pl.ds(c, rows, stride=MAX_DILATION), :]


def _in_proj(x2d, g, w_bf16, batch, seq):
    t = x2d.shape[0]
    tiles_per_seq = seq // ROW_TILE
    rows = ROW_TILE // MAX_DILATION
    return pl.pallas_call(
        _in_proj_kernel,
        grid=(t // ROW_TILE,),
        in_specs=[
            pl.BlockSpec((ROW_TILE, D_MODEL), lambda i: (i, 0)),
            pl.BlockSpec((1, D_MODEL), lambda i: (0, 0)),
            pl.BlockSpec((D_MODEL, IN_WIDTH), lambda i: (0, 0), pipeline_mode=pl.Buffered(1)),
        ],
        out_specs=[
            pl.BlockSpec((1, QKV_SLABS, MAX_DILATION, rows, LANES),
                         lambda i: (i // tiles_per_seq, 0, 0, i % tiles_per_seq, 0)),
            pl.BlockSpec((ROW_TILE, 2 * LRU_WIDTH), lambda i: (i, 0)),
        ],
        out_shape=[
            jax.ShapeDtypeStruct((batch, QKV_SLABS, MAX_DILATION, seq // MAX_DILATION, LANES), f32),
            jax.ShapeDtypeStruct((t, 2 * LRU_WIDTH), f32),
        ],
        scratch_shapes=[pltpu.VMEM((QKV_SLABS, ROW_TILE, LANES), f32)],
        compiler_params=pltpu.CompilerParams(
            dimension_semantics=("parallel",), vmem_limit_bytes=VMEM_LIMIT),
        name="in_proj",
    )(x2d, g, w_bf16)


def _chunking(d):
    n_chunks = MAX_DILATION // d
    return n_chunks, SPAN // n_chunks


def _block_rows(d, n):
    _, chunk_rows = _chunking(d)
    start = n * chunk_rows
    if not isinstance(start, int):
        start = pl.multiple_of(start, chunk_rows)
    return pl.ds(start, chunk_rows)


def _load_block(ref, d, cls, n):
    n_chunks, _ = _chunking(d)
    rows = _block_rows(d, n)
    parts = [ref[0, 0, cls + d * j, rows, :] for j in range(n_chunks)]
    return parts[0] if n_chunks == 1 else jnp.concatenate(parts, axis=0)


def _store_block(ref, p, d, cls, n, val):
    n_chunks, chunk_rows = _chunking(d)
    rows = _block_rows(d, n)
    for j in range(n_chunks):
        ref[p, cls + d * j, rows, :] = val[j * chunk_rows:(j + 1) * chunk_rows]


def _block_position(r, d):
    n_chunks, chunk_rows = _chunking(d)
    shift = chunk_rows.bit_length() - 1
    return n_chunks * (r & (chunk_rows - 1)) + (r >> shift)


def _attn_kernel(q_ref, k_ref, v_ref, o_ref, num_ref, den_ref, mx_ref, band_ref, first_ref):
    lane = lax.broadcasted_iota(jnp.int32, (1, LANES), 1)
    head_sel = (lane < HEAD_DIM, lane >= HEAD_DIM)

    def ones_sel(h, rows):
        keep = lax.broadcasted_iota(jnp.int32, (rows, LANES), 1) < HEAD_DIM
        return jnp.where(keep if h == 0 else ~keep, 1.0, 0.0).astype(bf16)

    nt = (((1,), (1,)), ((), ()))

    for p, d in enumerate(DILATIONS):
        r = lax.broadcasted_iota(jnp.int32, (SPAN, 2 * SPAN), 0)
        col = lax.broadcasted_iota(jnp.int32, (SPAN, 2 * SPAN), 1)
        key_pos = jnp.where(col < SPAN, _block_position(col, d) - SPAN,
                            _block_position(col - SPAN, d))
        dist = _block_position(r, d) - key_pos
        band_ref[p] = jnp.where((dist >= 0) & (dist <= SPAN), 0.0, -jnp.inf)
        r1 = lax.broadcasted_iota(jnp.int32, (SPAN, SPAN), 0)
        c1 = lax.broadcasted_iota(jnp.int32, (SPAN, SPAN), 1)
        first_ref[p] = jnp.where(_block_position(r1, d) >= _block_position(c1, d), 0.0, -jnp.inf)

    def class_units(p, d, cls, n0, count, has_first):
        lo = n0 if has_first else n0 - 1
        nblk = count if has_first else count + 1
        kb = [_load_block(k_ref, d, cls, lo + i).astype(bf16) for i in range(nblk)]
        vb = [_load_block(v_ref, d, cls, lo + i).astype(bf16) for i in range(nblk)]
        units = []
        for u in range(count):
            n = n0 + u
            q = (_load_block(q_ref, d, cls, n) * Q_SCALE).astype(bf16)
            if has_first and u == 0:
                k, v, bias = kb[0], vb[0], first_ref[p]
            else:
                i = u if has_first else u + 1
                k = jnp.concatenate([kb[i - 1], kb[i]], axis=0)
                v = jnp.concatenate([vb[i - 1], vb[i]], axis=0)
                bias = band_ref[p]
            units.append((q, k, v, bias, (p, d, cls, n)))
        return units

    def run(units):
        scores = []
        for q, k, _, bias, _ in units:
            for sel in head_sel:
                km = jnp.where(sel, k, jnp.zeros_like(k))
                scores.append(lax.dot_general(q, km, nt, preferred_element_type=f32) + bias)
        probs, maxes = [], []
        for s in scores:
            mx = jnp.max(s, axis=-1, keepdims=True)
            probs.append(jnp.exp2(s - mx).astype(bf16))
            maxes.append(mx)
        for idx, (_, _, v, _, where) in enumerate(units):
            acc = None
            for h, sel in enumerate(head_sel):
                vm = jnp.where(sel, v, jnp.zeros_like(v))
                vaug = jnp.concatenate([vm, ones_sel(h, v.shape[0])], axis=1)
                part = jnp.dot(probs[2 * idx + h], vaug, preferred_element_type=f32)
                acc = part if acc is None else acc + part
            p, d, cls, n = where
            _store_block(num_ref, p, d, cls, n, acc[:, :LANES])
            _store_block(den_ref, p, d, cls, n, acc[:, LANES:])
            _store_block(mx_ref, p, d, cls, n,
                         jnp.where(head_sel[0], maxes[2 * idx], maxes[2 * idx + 1]))

    blocks_per_class = [q_ref.shape[3] * MAX_DILATION // (d * SPAN) for d in DILATIONS]

    run(class_units(0, 1, 0, 0, ATTN_GROUP, True))

    def d1_group(g, carry):
        run(class_units(0, 1, 0, g * ATTN_GROUP, ATTN_GROUP, False))
        return carry

    lax.fori_loop(1, blocks_per_class[0] // ATTN_GROUP, d1_group, 0)

    def d4_group(c, carry):
        run(class_units(1, 4, c, 0, blocks_per_class[1], True))
        return carry

    lax.fori_loop(0, 4, d4_group, 0)

    classes_per_step = ATTN_GROUP // blocks_per_class[2]

    def d16_group(g, carry):
        units = []
        for cc in range(classes_per_step):
            units += class_units(2, 16, g * classes_per_step + cc, 0, blocks_per_class[2], True)
        run(units)
        return carry

    lax.fori_loop(0, MAX_DILATION // classes_per_step, d16_group, 0)

    def merge(c, carry):
        m0, m1, m2 = mx_ref[0, c], mx_ref[1, c], mx_ref[2, c]
        top = jnp.maximum(jnp.maximum(m0, m1), m2)
        w0, w1, w2 = jnp.exp2(m0 - top), jnp.exp2(m1 - top), jnp.exp2(m2 - top)
        num = w0 * num_ref[0, c] + w1 * num_ref[1, c] + w2 * num_ref[2, c]
        den = w0 * den_ref[0, c] + w1 * den_ref[1, c] + w2 * den_ref[2, c]
        o_ref[0, pl.ds(c, q_ref.shape[3], stride=MAX_DILATION), :] = num / den
        return carry

    lax.fori_loop(0, MAX_DILATION, merge, 0)


def _attention(qkv):
    b, _, _, rows, _ = qkv.shape
    s = rows * MAX_DILATION
    assert all(s % (d * SPAN) == 0 for d in DILATIONS)
    assert (s // SPAN) % ATTN_GROUP == 0 and ATTN_GROUP % (s // (MAX_DILATION * SPAN)) == 0
    blk = (1, 1, MAX_DILATION, rows, LANES)
    stat = pltpu.VMEM((len(DILATIONS), MAX_DILATION, rows, LANES), f32)
    return pl.pallas_call(
        _attn_kernel,
        grid=(b, HEAD_PAIRS),
        in_specs=[
            pl.BlockSpec(blk, lambda i, j: (i, j, 0, 0, 0)),
            pl.BlockSpec(blk, lambda i, j: (i, HEAD_PAIRS + j, 0, 0, 0)),
            pl.BlockSpec(blk, lambda i, j: (i, 2 * HEAD_PAIRS + j, 0, 0, 0)),
        ],
        out_specs=pl.BlockSpec((1, s, LANES), lambda i, j: (i, 0, j)),
        out_shape=jax.ShapeDtypeStruct((b, s, ATTN_WIDTH), f32),
        scratch_shapes=[
            stat, stat, stat,
            pltpu.VMEM((len(DILATIONS), SPAN, 2 * SPAN), f32),
            pltpu.VMEM((len(DILATIONS), SPAN, SPAN), f32),
        ],
        compiler_params=pltpu.CompilerParams(
            dimension_semantics=("parallel", "parallel"), vmem_limit_bytes=VMEM_LIMIT),
        name="dilated_attention",
    )(qkv, qkv, qkv)


def _rglru_kernel(xr_ref, xprev_ref, gr_ref, cw_ref, cb_ref, wg_ref, br_ref, bi_ref, lam_ref,
                  o_ref, xe_ref, a_ref, u_ref, h_ref, *, batch):
    tc = LRU_TIME_CHUNK
    step = pl.program_id(0)

    @pl.when(step == 0)
    def _():
        h_ref[...] = jnp.zeros_like(h_ref)

    neg_lam = -lam_ref[...]
    softplus = jnp.maximum(neg_lam, 0.0) + jnp.log1p(jnp.exp(-jnp.abs(neg_lam)))
    rate = -LRU_C * softplus
    half = LRU_WIDTH // 2

    def dense(b, carry):
        prev = xprev_ref[b]
        xe_ref[0:SUBLANES, :] = jnp.where(step == 0, 0.0, prev)
        xe_ref[SUBLANES:, :] = xr_ref[b]
        y = cb_ref[...] + cw_ref[CONV_WIDTH - 1:CONV_WIDTH, :] * xe_ref[SUBLANES:, :]
        for j in range(1, CONV_WIDTH):
            tap = CONV_WIDTH - 1 - j
            y = y + cw_ref[tap:tap + 1, :] * xe_ref[pl.ds(SUBLANES - j, tc), :]
        yb = y.astype(bf16)
        gates = [jnp.dot(yb[:, hf * half:(hf + 1) * half], wg_ref[hf],
                         preferred_element_type=f32) for hf in range(2)]
        pre_r = jnp.concatenate([gates[0][:, :half], gates[1][:, :half]], axis=-1)
        pre_i = jnp.concatenate([gates[0][:, half:], gates[1][:, half:]], axis=-1)
        r = jax.nn.sigmoid(pre_r + br_ref[...])
        i = jax.nn.sigmoid(pre_i + bi_ref[...])
        log_a = rate * r
        a = jnp.exp(log_a)
        u = jnp.sqrt(-jnp.tanh(log_a) * (a * a + 1.0)) * (i * y)
        for g in range(LRU_GROUPS):
            cols = slice(g * LANES, (g + 1) * LANES)
            a_ref[g, pl.ds(b, tc, stride=batch), :] = a[:, cols]
            u_ref[g, pl.ds(b, tc, stride=batch), :] = u[:, cols]
        return carry

    lax.fori_loop(0, batch, dense, 0)

    def scan(t, hs):
        rows = pl.ds(pl.multiple_of(t * batch, batch), batch)
        new = []
        for g in range(LRU_GROUPS):
            h = a_ref[g, rows, :] * hs[g] + u_ref[g, rows, :]
            u_ref[g, rows, :] = h
            new.append(h)
        return tuple(new)

    hs = lax.fori_loop(0, tc, scan, tuple(h_ref[g] for g in range(LRU_GROUPS)), unroll=8)
    for g in range(LRU_GROUPS):
        h_ref[g] = hs[g]

    def emit(b, carry):
        h = jnp.concatenate(
            [u_ref[g, pl.ds(b, tc, stride=batch), :] for g in range(LRU_GROUPS)], axis=-1)
        o_ref[b] = h * jax.nn.gelu(gr_ref[b])
        return carry

    lax.fori_loop(0, batch, emit, 0)


def _rglru(zr3, conv_w, conv_b, w_gates, b_r, b_i, lam):
    b, s, _ = zr3.shape
    assert b == SUBLANES
    tc = LRU_TIME_CHUNK
    blk = (b, tc, LRU_WIDTH)
    vec = pl.BlockSpec((1, LRU_WIDTH), lambda i: (0, 0))
    return pl.pallas_call(
        functools.partial(_rglru_kernel, batch=b),
        grid=(s // tc,),
        in_specs=[
            pl.BlockSpec(blk, lambda i: (0, i, 0)),
            pl.BlockSpec((b, SUBLANES, LRU_WIDTH),
                         lambda i: (0, jnp.maximum(i * (tc // SUBLANES) - 1, 0), 0)),
            pl.BlockSpec(blk, lambda i: (0, i, 1)),
            pl.BlockSpec((CONV_WIDTH, LRU_WIDTH), lambda i: (0, 0)),
            vec,
            pl.BlockSpec((2, LRU_WIDTH // 2, LRU_WIDTH), lambda i: (0, 0, 0)),
            vec, vec, vec,
        ],
        out_specs=pl.BlockSpec(blk, lambda i: (0, i, 0)),
        out_shape=jax.ShapeDtypeStruct((b, s, LRU_WIDTH), f32),
        scratch_shapes=[
            pltpu.VMEM((tc + SUBLANES, LRU_WIDTH), f32),
            pltpu.VMEM((LRU_GROUPS, tc * b, LANES), f32),
            pltpu.VMEM((LRU_GROUPS, tc * b, LANES), f32),
            pltpu.VMEM((LRU_GROUPS, b, LANES), f32),
        ],
        compiler_params=pltpu.CompilerParams(
            dimension_semantics=("arbitrary",), vmem_limit_bytes=VMEM_LIMIT),
        name="rglru",
    )(zr3, zr3, zr3, conv_w, conv_b, w_gates, b_r, b_i, lam)


def _gate_weights(w_r, w_i):
    def half_dense(w, hf):
        blocks = w[hf * 4:(hf + 1) * 4]
        return jax.scipy.linalg.block_diag(*[blocks[n] for n in range(4)])
    return jnp.stack([
        jnp.concatenate([half_dense(w_r, hf), half_dense(w_i, hf)], axis=-1) for hf in range(2)
    ]).astype(bf16)


def _out_mlp_kernel(attn_ref, rec_ref, x_ref, gpost_ref, gpre2_ref, gpost2_ref,
                    wo_ref, w1_ref, w2_ref, o_ref):
    mix = jnp.concatenate([attn_ref[...].astype(bf16), rec_ref[...].astype(bf16)], axis=-1)
    m = jnp.dot(mix, wo_ref[...], preferred_element_type=f32)
    x1 = x_ref[...] + _rms(m, gpost_ref[...])
    h = _rms(x1, gpre2_ref[...]).astype(bf16)
    y = None
    for n in range(D_FF // (2 * N_CHUNK)):
        cols = slice(n * 2 * N_CHUNK, (n + 1) * 2 * N_CHUNK)
        a = jnp.maximum(jnp.dot(h, w1_ref[:, cols], preferred_element_type=f32), 0.0)
        part = jnp.dot((a * a).astype(bf16), w2_ref[cols, :], preferred_element_type=f32)
        y = part if y is None else y + part
    o_ref[...] = x1 + _rms(y, gpost2_ref[...])


def _out_mlp(attn2d, rec2d, x2d, g_post, g_pre2, g_post2, wo, w1, w2):
    t = x2d.shape[0]
    vec = pl.BlockSpec((1, D_MODEL), lambda i: (0, 0))
    const = lambda shape: pl.BlockSpec(shape, lambda i: (0, 0), pipeline_mode=pl.Buffered(1))
    return pl.pallas_call(
        _out_mlp_kernel,
        grid=(t // ROW_TILE,),
        in_specs=[
            pl.BlockSpec((ROW_TILE, ATTN_WIDTH), lambda i: (i, 0)),
            pl.BlockSpec((ROW_TILE, LRU_WIDTH), lambda i: (i, 0)),
            pl.BlockSpec((ROW_TILE, D_MODEL), lambda i: (i, 0)),
            vec, vec, vec,
            const((ATTN_WIDTH + LRU_WIDTH, D_MODEL)),
            const((D_MODEL, D_FF)),
            const((D_FF, D_MODEL)),
        ],
        out_specs=pl.BlockSpec((ROW_TILE, D_MODEL), lambda i: (i, 0)),
        out_shape=jax.ShapeDtypeStruct((t, D_MODEL), f32),
        compiler_params=pltpu.CompilerParams(
            dimension_semantics=("parallel",), vmem_limit_bytes=VMEM_LIMIT),
        name="out_mlp",
    )(attn2d, rec2d, x2d, g_post, g_pre2, g_post2, wo, w1, w2)


def kernel(x, mix_norm_pre, mix_norm_post, mlp_norm_pre, mlp_norm_post, w_in, conv_w, conv_b,
           w_rgate, b_rgate, w_igate, b_igate, lru_lambda, w_out, w_ff_in, w_ff_out):
    b, s, d = x.shape
    depth = w_in.shape[0]
    t = b * s
    x2d = x.reshape(t, d)
    row = lambda v: v.reshape(1, -1)
    for l in range(depth):
        qkv, zr = _in_proj(x2d, row(mix_norm_pre[l]), w_in[l].astype(bf16), b, s)
        attn = _attention(qkv)
        rec = _rglru(zr.reshape(b, s, 2 * LRU_WIDTH), conv_w[l], row(conv_b[l]),
                     _gate_weights(w_rgate[l], w_igate[l]),
                     row(b_rgate[l]), row(b_igate[l]), row(lru_lambda[l]))
        x2d = _out_mlp(attn.reshape(t, ATTN_WIDTH), rec.reshape(t, LRU_WIDTH), x2d,
                       row(mix_norm_post[l]), row(mlp_norm_pre[l]), row(mlp_norm_post[l]),
                       w_out[l].astype(bf16), w_ff_in[l].astype(bf16), w_ff_out[l].astype(bf16))
    return x2d.reshape(b, s, d)
```

```python
import functools
import math

import jax
import jax.numpy as jnp
from jax import lax
from jax.experimental import pallas as pl
from jax.experimental.pallas import tpu as pltpu

D_MODEL = 1024
N_HEADS = 8
HEAD_DIM = 64
ATTN_WIDTH = N_HEADS * HEAD_DIM
DILATIONS = (1, 4, 16)
MAX_DILATION = 16
SPAN = 128
LRU_WIDTH = 512
N_LRU_BLOCKS = 8
LRU_BLOCK = LRU_WIDTH // N_LRU_BLOCKS
CONV_WIDTH = 4
LRU_C = 8.0
IN_WIDTH = 3 * ATTN_WIDTH + 2 * LRU_WIDTH
D_FF = 4 * D_MODEL
NORM_EPS = 1e-6

LANES = 128
SUBLANES = 8
VMEM_LIMIT = 56 * 1024 * 1024

ROW_TILE = 512
N_CHUNK = 512
LRU_TIME_CHUNK = 256
LRU_DENSE_ROWS = 256
HEAD_PAIRS = ATTN_WIDTH // LANES
QKV_SLABS = 3 * HEAD_PAIRS
LRU_GROUPS = LRU_WIDTH // LANES
ATTN_GROUP = 32
Q_SCALE = HEAD_DIM ** -0.5 * math.log2(math.e)

bf16 = jnp.bfloat16
f32 = jnp.float32


def _rms(x, g):
    y = x * lax.rsqrt(jnp.mean(x * x, axis=-1, keepdims=True) + NORM_EPS)
    return y * g


def _in_proj_kernel(x_ref, g_ref, w_ref, qkv_ref, zr_ref):
    h = _rms(x_ref[...], g_ref[...]).astype(bf16)
    slabs_per_chunk = N_CHUNK // LANES
    qkv_chunks = QKV_SLABS // slabs_per_chunk
    for n in range(IN_WIDTH // N_CHUNK):
        z = jnp.dot(h, w_ref[:, n * N_CHUNK:(n + 1) * N_CHUNK], preferred_element_type=f32)
        if n < qkv_chunks:
            for s in range(slabs_per_chunk):
                qkv_ref[0, n * slabs_per_chunk + s] = z[:, s * LANES:(s + 1) * LANES]
        else:
            m = n - qkv_chunks
            zr_ref[:, m * N_CHUNK:(m + 1) * N_CHUNK] = z


def _in_proj(x2d, g, w_bf16, batch, seq):
    t = x2d.shape[0]
    tiles_per_seq = seq // ROW_TILE
    return pl.pallas_call(
        _in_proj_kernel,
        grid=(t // ROW_TILE,),
        in_specs=[
            pl.BlockSpec((ROW_TILE, D_MODEL), lambda i: (i, 0)),
            pl.BlockSpec((1, D_MODEL), lambda i: (0, 0)),
            pl.BlockSpec((D_MODEL, IN_WIDTH), lambda i: (0, 0), pipeline_mode=pl.Buffered(1)),
        ],
        out_specs=[
            pl.BlockSpec((1, QKV_SLABS, ROW_TILE, LANES),
                         lambda i: (i // tiles_per_seq, 0, i % tiles_per_seq, 0)),
            pl.BlockSpec((ROW_TILE, 2 * LRU_WIDTH), lambda i: (i, 0)),
        ],
        out_shape=[
            jax.ShapeDtypeStruct((batch, QKV_SLABS, seq, LANES), f32),
            jax.ShapeDtypeStruct((t, 2 * LRU_WIDTH), f32),
        ],
        compiler_params=pltpu.CompilerParams(
            dimension_semantics=("parallel",), vmem_limit_bytes=VMEM_LIMIT),
        name="in_proj",
    )(x2d, g, w_bf16)


def _rows(start, size, stride=1):
    if stride == 1:
        if not isinstance(start, int):
            start = pl.multiple_of(start, SPAN)
        return pl.ds(start, size)
    return pl.ds(start, size, stride=stride)


def _attn_kernel(q_ref, k_ref, v_ref, o_ref, num_ref, den_ref, mx_ref, q4_ref, k4_ref, v4_ref,
                 band_ref, first_ref):
    seq = q_ref.shape[2]
    slab_rows = seq // 4
    lane = lax.broadcasted_iota(jnp.int32, (1, LANES), 1)
    head_sel = (lane < HEAD_DIM, lane >= HEAD_DIM)
    nt = (((1,), (1,)), ((), ()))

    def ones_sel(h, rows):
        keep = lax.broadcasted_iota(jnp.int32, (rows, LANES), 1) < HEAD_DIM
        return jnp.where(keep if h == 0 else ~keep, 1.0, 0.0).astype(bf16)

    dist = (SPAN + lax.broadcasted_iota(jnp.int32, (SPAN, 2 * SPAN), 0)
            - lax.broadcasted_iota(jnp.int32, (SPAN, 2 * SPAN), 1))
    band_ref[...] = jnp.where((dist >= 0) & (dist <= SPAN), 0.0, -jnp.inf)
    first_ref[...] = jnp.where(lax.broadcasted_iota(jnp.int32, (SPAN, SPAN), 0)
                               >= lax.broadcasted_iota(jnp.int32, (SPAN, SPAN), 1), 0.0, -jnp.inf)

    def make_units(qall, kall, vall, count, has_first, sinks):
        qall = (qall * Q_SCALE).astype(bf16)
        kall, vall = kall.astype(bf16), vall.astype(bf16)
        units = []
        for u in range(count):
            q = qall[u * SPAN:(u + 1) * SPAN]
            if has_first and u == 0:
                k, v, bias = kall[:SPAN], vall[:SPAN], first_ref
            else:
                i = u if has_first else u + 1
                k, v = kall[(i - 1) * SPAN:(i + 1) * SPAN], vall[(i - 1) * SPAN:(i + 1) * SPAN]
                bias = band_ref
            units.append((q, k, v, bias, sinks[u]))
        return units

    def run(units):
        scores = []
        for q, k, _, bias, _ in units:
            for sel in head_sel:
                km = jnp.where(sel, k, jnp.zeros_like(k))
                scores.append(lax.dot_general(q, km, nt, preferred_element_type=f32) + bias[...])
        probs, maxes = [], []
        for s in scores:
            mx = jnp.max(s, axis=-1, keepdims=True)
            probs.append(jnp.exp2(s - mx).astype(bf16))
            maxes.append(mx)
        for idx, (_, _, v, _, (p, slab, rows)) in enumerate(units):
            acc = None
            for h, sel in enumerate(head_sel):
                vm = jnp.where(sel, v, jnp.zeros_like(v))
                vaug = jnp.concatenate([vm, ones_sel(h, v.shape[0])], axis=1)
                part = jnp.dot(probs[2 * idx + h], vaug, preferred_element_type=f32)
                acc = part if acc is None else acc + part
            num_ref[p, slab, rows, :] = acc[:, :LANES]
            den_ref[p, slab, rows, :] = acc[:, LANES:]
            mx_ref[p, slab, rows, :] = jnp.where(head_sel[0], maxes[2 * idx], maxes[2 * idx + 1])

    def d1_group(n0, has_first):
        lo = n0 if has_first else n0 - 1
        nk = ATTN_GROUP if has_first else ATTN_GROUP + 1
        sinks = []
        for u in range(ATTN_GROUP):
            start = (n0 + u) * SPAN
            sinks.append((0, start // slab_rows, _rows(start % slab_rows, SPAN)))
        run(make_units(q_ref[0, 0, _rows(n0 * SPAN, ATTN_GROUP * SPAN), :],
                       k_ref[0, 0, _rows(lo * SPAN, nk * SPAN), :],
                       v_ref[0, 0, _rows(lo * SPAN, nk * SPAN), :],
                       ATTN_GROUP, has_first, sinks))

    d1_group(0, True)

    def d1_step(g, carry):
        d1_group(g * ATTN_GROUP, False)
        return carry

    lax.fori_loop(1, seq // (SPAN * ATTN_GROUP), d1_step, 0)

    blocks4 = slab_rows // SPAN
    classes4 = ATTN_GROUP // blocks4

    def d4_step(g, carry):
        units = []
        for cc in range(classes4):
            c = g * classes4 + cc
            qall = q_ref[0, 0, _rows(c, slab_rows, 4), :]
            kall = k_ref[0, 0, _rows(c, slab_rows, 4), :]
            vall = v_ref[0, 0, _rows(c, slab_rows, 4), :]
            q4_ref[c], k4_ref[c], v4_ref[c] = qall, kall, vall
            sinks = [(1, c, _rows(u * SPAN, SPAN)) for u in range(blocks4)]
            units += make_units(qall, kall, vall, blocks4, True, sinks)
        run(units)
        return carry

    lax.fori_loop(0, 4 // classes4, d4_step, 0)

    blocks16 = slab_rows // (4 * SPAN)
    classes16 = ATTN_GROUP // blocks16

    def d16_step(g, carry):
        units = []
        for cc in range(classes16):
            c16 = g * classes16 + cc
            c4, j = c16 % 4, c16 // 4
            rows = _rows(j, blocks16 * SPAN, 4)
            sinks = [(2, c4, _rows(j + 4 * SPAN * u, SPAN, 4)) for u in range(blocks16)]
            units += make_units(q4_ref[c4, rows, :], k4_ref[c4, rows, :], v4_ref[c4, rows, :],
                                blocks16, True, sinks)
        run(units)
        return carry

    lax.fori_loop(0, MAX_DILATION // classes16, d16_step, 0)

    merge_rows = 256
    per_slab = slab_rows // merge_rows

    def merge(i, carry):
        c4 = i // per_slab
        m0 = (i - c4 * per_slab) * merge_rows
        rows = _rows(m0, merge_rows)
        tok = c4 + 4 * m0
        trows = _rows(tok % slab_rows, merge_rows, 4)
        tslab = tok // slab_rows

        def stat(ref, p):
            return ref[p, tslab, trows, :] if p == 0 else ref[p, c4, rows, :]

        mx = [stat(mx_ref, p) for p in range(3)]
        top = jnp.maximum(jnp.maximum(mx[0], mx[1]), mx[2])
        w = [jnp.exp2(m - top) for m in mx]
        num = sum(w[p] * stat(num_ref, p) for p in range(3))
        den = sum(w[p] * stat(den_ref, p) for p in range(3))
        o_ref[0, _rows(tok, merge_rows, 4), :] = num / den
        return carry

    lax.fori_loop(0, 4 * per_slab, merge, 0)


def _attention(qkv):
    b, _, s, _ = qkv.shape
    assert s % (MAX_DILATION * SPAN) == 0 and (s // SPAN) % ATTN_GROUP == 0
    assert ATTN_GROUP % (s // (4 * SPAN)) == 0 and 4 % (ATTN_GROUP // (s // (4 * SPAN))) == 0
    blk = (1, 1, s, LANES)
    stat = pltpu.VMEM((len(DILATIONS), 4, s // 4, LANES), f32)
    mod4 = pltpu.VMEM((4, s // 4, LANES), f32)
    return pl.pallas_call(
        _attn_kernel,
        grid=(b, HEAD_PAIRS),
        in_specs=[
            pl.BlockSpec(blk, lambda i, j: (i, j, 0, 0)),
            pl.BlockSpec(blk, lambda i, j: (i, HEAD_PAIRS + j, 0, 0)),
            pl.BlockSpec(blk, lambda i, j: (i, 2 * HEAD_PAIRS + j, 0, 0)),
        ],
        out_specs=pl.BlockSpec((1, s, LANES), lambda i, j: (i, 0, j)),
        out_shape=jax.ShapeDtypeStruct((b, s, ATTN_WIDTH), f32),
        scratch_shapes=[
            stat, stat, stat, mod4, mod4, mod4,
            pltpu.VMEM((SPAN, 2 * SPAN), f32),
            pltpu.VMEM((SPAN, SPAN), f32),
        ],
        compiler_params=pltpu.CompilerParams(
            dimension_semantics=("parallel", "parallel"), vmem_limit_bytes=VMEM_LIMIT),
        name="dilated_attention",
    )(qkv, qkv, qkv)


def _sigmoid(x):
    return 0.5 * jnp.tanh(0.5 * x) + 0.5


def _rglru_kernel(xr_ref, gr_ref, cw_ref, cb_ref, wg_ref, br_ref, bi_ref, lam_ref,
                  o_ref, xi_ref, a_ref, u_ref, h_ref, *, batch):
    tc = LRU_TIME_CHUNK
    halo = (CONV_WIDTH - 1) * batch
    step = pl.program_id(0)

    @pl.when(step == 0)
    def _():
        h_ref[...] = jnp.zeros_like(h_ref)
        xi_ref[:, 0:halo, :] = jnp.zeros((LRU_GROUPS, halo, LANES), f32)

    def interleave(b, carry):
        x = xr_ref[b]
        for g in range(LRU_GROUPS):
            xi_ref[g, pl.ds(halo + b, tc, stride=batch), :] = x[:, g * LANES:(g + 1) * LANES]
        return carry

    lax.fori_loop(0, batch, interleave, 0)

    neg_lam = -lam_ref[...]
    softplus = jnp.maximum(neg_lam, 0.0) + jnp.log1p(jnp.exp(-jnp.abs(neg_lam)))
    rate = -LRU_C * softplus
    half = LRU_WIDTH // 2
    chunk = LRU_DENSE_ROWS

    def dense(ci, carry):
        r0 = pl.multiple_of(ci * chunk, chunk)
        y = cb_ref[...]
        for j in range(CONV_WIDTH):
            tap = CONV_WIDTH - 1 - j
            rows = pl.ds(pl.multiple_of(r0 + halo - j * batch, batch), chunk)
            xs = jnp.concatenate([xi_ref[g, rows, :] for g in range(LRU_GROUPS)], axis=-1)
            y = y + cw_ref[tap:tap + 1, :] * xs
        yb = y.astype(bf16)
        gates = [jnp.dot(yb[:, hf * half:(hf + 1) * half], wg_ref[hf],
                         preferred_element_type=f32) for hf in range(2)]
        pre_r = jnp.concatenate([gates[0][:, :half], gates[1][:, :half]], axis=-1)
        pre_i = jnp.concatenate([gates[0][:, half:], gates[1][:, half:]], axis=-1)
        r = _sigmoid(pre_r + br_ref[...])
        i = _sigmoid(pre_i + bi_ref[...])
        log_a = rate * r
        a = jnp.exp(log_a)
        u = jnp.sqrt(-jnp.tanh(log_a) * (a * a + 1.0)) * (i * y)
        for g in range(LRU_GROUPS):
            cols = slice(g * LANES, (g + 1) * LANES)
            a_ref[g, pl.ds(r0, chunk), :] = a[:, cols]
            u_ref[g, pl.ds(r0, chunk), :] = u[:, cols]
        return carry

    lax.fori_loop(0, tc * batch // chunk, dense, 0)
    xi_ref[:, 0:halo, :] = xi_ref[:, tc * batch:tc * batch + halo, :]

    def scan(t, hs):
        rows = pl.ds(pl.multiple_of(t * batch, batch), batch)
        new = []
        for g in range(LRU_GROUPS):
            h = a_ref[g, rows, :] * hs[g] + u_ref[g, rows, :]
            u_ref[g, rows, :] = h
            new.append(h)
        return tuple(new)

    hs = lax.fori_loop(0, tc, scan, tuple(h_ref[g] for g in range(LRU_GROUPS)), unroll=8)
    for g in range(LRU_GROUPS):
        h_ref[g] = hs[g]

    def emit(b, carry):
        h = jnp.concatenate(
            [u_ref[g, pl.ds(b, tc, stride=batch), :] for g in range(LRU_GROUPS)], axis=-1)
        o_ref[b] = h * jax.nn.gelu(gr_ref[b])
        return carry

    lax.fori_loop(0, batch, emit, 0)


def _rglru(zr3, conv_w, conv_b, w_gates, b_r, b_i, lam):
    b, s, _ = zr3.shape
    assert b == SUBLANES
    tc = LRU_TIME_CHUNK
    blk = (b, tc, LRU_WIDTH)
    vec = pl.BlockSpec((1, LRU_WIDTH), lambda i: (0, 0))
    return pl.pallas_call(
        functools.partial(_rglru_kernel, batch=b),
        grid=(s // tc,),
        in_specs=[
            pl.BlockSpec(blk, lambda i: (0, i, 0)),
            pl.BlockSpec(blk, lambda i: (0, i, 1)),
            pl.BlockSpec((CONV_WIDTH, LRU_WIDTH), lambda i: (0, 0)),
            vec,
            pl.BlockSpec((2, LRU_WIDTH // 2, LRU_WIDTH), lambda i: (0, 0, 0)),
            vec, vec, vec,
        ],
        out_specs=pl.BlockSpec(blk, lambda i: (0, i, 0)),
        out_shape=jax.ShapeDtypeStruct((b, s, LRU_WIDTH), f32),
        scratch_shapes=[
            pltpu.VMEM((LRU_GROUPS, (tc + CONV_WIDTH - 1) * b, LANES), f32),
            pltpu.VMEM((LRU_GROUPS, tc * b, LANES), f32),
            pltpu.VMEM((LRU_GROUPS, tc * b, LANES), f32),
            pltpu.VMEM((LRU_GROUPS, b, LANES), f32),
        ],
        compiler_params=pltpu.CompilerParams(
            dimension_semantics=("arbitrary",), vmem_limit_bytes=VMEM_LIMIT),
        name="rglru",
    )(zr3, zr3, conv_w, conv_b, w_gates, b_r, b_i, lam)


def _gate_weights(w_r, w_i):
    def half_dense(w, hf):
        blocks = w[hf * 4:(hf + 1) * 4]
        return jax.scipy.linalg.block_diag(*[blocks[n] for n in range(4)])
    return jnp.stack([
        jnp.concatenate([half_dense(w_r, hf), half_dense(w_i, hf)], axis=-1) for hf in range(2)
    ]).astype(bf16)


def _out_mlp_kernel(attn_ref, rec_ref, x_ref, gpost_ref, gpre2_ref, gpost2_ref,
                    wo_ref, w1_ref, w2_ref, o_ref):
    halves = [slice(i * ROW_TILE // 2, (i + 1) * ROW_TILE // 2) for i in range(2)]
    ff_chunk = 2 * N_CHUNK
    mix = [jnp.concatenate([attn_ref[r, :].astype(bf16), rec_ref[r, :].astype(bf16)], axis=-1)
           for r in halves]
    m = [jnp.dot(mx, wo_ref[...], preferred_element_type=f32) for mx in mix]
    x1, y = [], []
    for i, r in enumerate(halves):
        x1.append(x_ref[r, :] + _rms(m[i], gpost_ref[...]))
        h = _rms(x1[i], gpre2_ref[...]).astype(bf16)
        acc = None
        for n in range(D_FF // ff_chunk):
            cols = slice(n * ff_chunk, (n + 1) * ff_chunk)
            a = jnp.maximum(jnp.dot(h, w1_ref[:, cols], preferred_element_type=f32), 0.0)
            part = jnp.dot((a * a).astype(bf16), w2_ref[cols, :], preferred_element_type=f32)
            acc = part if acc is None else acc + part
        y.append(acc)
    for i, r in enumerate(halves):
        o_ref[r, :] = x1[i] + _rms(y[i], gpost2_ref[...])


def _out_mlp(attn2d, rec2d, x2d, g_post, g_pre2, g_post2, wo, w1, w2):
    t = x2d.shape[0]
    vec = pl.BlockSpec((1, D_MODEL), lambda i: (0, 0))
    const = lambda shape: pl.BlockSpec(shape, lambda i: (0, 0), pipeline_mode=pl.Buffered(1))
    return pl.pallas_call(
        _out_mlp_kernel,
        grid=(t // ROW_TILE,),
        in_specs=[
            pl.BlockSpec((ROW_TILE, ATTN_WIDTH), lambda i: (i, 0)),
            pl.BlockSpec((ROW_TILE, LRU_WIDTH), lambda i: (i, 0)),
            pl.BlockSpec((ROW_TILE, D_MODEL), lambda i: (i, 0)),
            vec, vec, vec,
            const((ATTN_WIDTH + LRU_WIDTH, D_MODEL)),
            const((D_MODEL, D_FF)),
            const((D_FF, D_MODEL)),
        ],
        out_specs=pl.BlockSpec((ROW_TILE, D_MODEL), lambda i: (i, 0)),
        out_shape=jax.ShapeDtypeStruct((t, D_MODEL), f32),
        compiler_params=pltpu.CompilerParams(
            dimension_semantics=("parallel",), vmem_limit_bytes=VMEM_LIMIT),
        name="out_mlp",
    )(attn2d, rec2d, x2d, g_post, g_pre2, g_post2, wo, w1, w2)


def kernel(x, mix_norm_pre, mix_norm_post, mlp_norm_pre, mlp_norm_post, w_in, conv_w, conv_b,
           w_rgate, b_rgate, w_igate, b_igate, lru_lambda, w_out, w_ff_in, w_ff_out):
    b, s, d = x.shape
    depth = w_in.shape[0]
    t = b * s
    x2d = x.reshape(t, d)
    row = lambda v: v.reshape(1, -1)
    for l in range(depth):
        qkv, zr = _in_proj(x2d, row(mix_norm_pre[l]), w_in[l].astype(bf16), b, s)
        attn = _attention(qkv)
        rec = _rglru(zr.reshape(b, s, 2 * LRU_WIDTH), conv_w[l], row(conv_b[l]),
                     _gate_weights(w_rgate[l], w_igate[l]),
                     row(b_rgate[l]), row(b_igate[l]), row(lru_lambda[l]))
        x2d = _out_mlp(attn.reshape(t, ATTN_WIDTH), rec.reshape(t, LRU_WIDTH), x2d,
                       row(mix_norm_post[l]), row(mlp_norm_pre[l]), row(mlp_norm_post[l]),
                       w_out[l].astype(bf16), w_ff_in[l].astype(bf16), w_ff_out[l].astype(bf16))
    return x2d.reshape(b, s, d)
```

```python
import functools
import math

import jax
import jax.numpy as jnp
from jax import lax
from jax.experimental import pallas as pl
from jax.experimental.pallas import tpu as pltpu

D_MODEL = 1024
N_HEADS = 8
HEAD_DIM = 64
ATTN_WIDTH = N_HEADS * HEAD_DIM
DILATIONS = (1, 4, 16)
MAX_DILATION = 16
SPAN = 128
LRU_WIDTH = 512
N_LRU_BLOCKS = 8
LRU_BLOCK = LRU_WIDTH // N_LRU_BLOCKS
CONV_WIDTH = 4
LRU_C = 8.0
IN_WIDTH = 3 * ATTN_WIDTH + 2 * LRU_WIDTH
D_FF = 4 * D_MODEL
NORM_EPS = 1e-6

LANES = 128
SUBLANES = 8
VMEM_LIMIT = 56 * 1024 * 1024

ROW_TILE = 512
IN_ROW_TILE = 1024
N_CHUNK = 512
LRU_TIME_CHUNK = 256
LRU_DENSE_ROWS = 256
HEAD_PAIRS = ATTN_WIDTH // LANES
QKV_SLABS = 3 * HEAD_PAIRS
LRU_GROUPS = LRU_WIDTH // LANES
ATTN_GROUP = 32
QK_LOOKAHEAD = 4
GATE_PRESCALE = 0.5
SQRT_FLOOR = 1e-37
Q_SCALE = HEAD_DIM ** -0.5 * math.log2(math.e)

bf16 = jnp.bfloat16
f32 = jnp.float32


def _rms(x, g):
    y = x * lax.rsqrt(jnp.mean(x * x, axis=-1, keepdims=True) + NORM_EPS)
    return y * g


def _in_proj_kernel(x_ref, g_ref, w_ref, qkv_ref, zr_ref):
    h = _rms(x_ref[...], g_ref[...]).astype(bf16)
    slabs_per_chunk = N_CHUNK // LANES
    qkv_chunks = QKV_SLABS // slabs_per_chunk
    for n in range(IN_WIDTH // N_CHUNK):
        z = jnp.dot(h, w_ref[:, n * N_CHUNK:(n + 1) * N_CHUNK], preferred_element_type=f32)
        if n < qkv_chunks:
            for s in range(slabs_per_chunk):
                qkv_ref[0, n * slabs_per_chunk + s] = z[:, s * LANES:(s + 1) * LANES]
        else:
            m = n - qkv_chunks
            zr_ref[:, m * N_CHUNK:(m + 1) * N_CHUNK] = z


def _in_proj(x2d, g, w_bf16, batch, seq):
    t = x2d.shape[0]
    tiles_per_seq = seq // IN_ROW_TILE
    return pl.pallas_call(
        _in_proj_kernel,
        grid=(t // IN_ROW_TILE,),
        in_specs=[
            pl.BlockSpec((IN_ROW_TILE, D_MODEL), lambda i: (i, 0)),
            pl.BlockSpec((1, D_MODEL), lambda i: (0, 0)),
            pl.BlockSpec((D_MODEL, IN_WIDTH), lambda i: (0, 0), pipeline_mode=pl.Buffered(1)),
        ],
        out_specs=[
            pl.BlockSpec((1, QKV_SLABS, IN_ROW_TILE, LANES),
                         lambda i: (i // tiles_per_seq, 0, i % tiles_per_seq, 0)),
            pl.BlockSpec((IN_ROW_TILE, 2 * LRU_WIDTH), lambda i: (i, 0)),
        ],
        out_shape=[
            jax.ShapeDtypeStruct((batch, QKV_SLABS, seq, LANES), f32),
            jax.ShapeDtypeStruct((t, 2 * LRU_WIDTH), f32),
        ],
        compiler_params=pltpu.CompilerParams(
            dimension_semantics=("parallel",), vmem_limit_bytes=VMEM_LIMIT),
        name="in_proj",
    )(x2d, g, w_bf16)


def _rows(start, size, stride=1):
    if stride == 1:
        if not isinstance(start, int):
            start = pl.multiple_of(start, SPAN)
        return pl.ds(start, size)
    return pl.ds(start, size, stride=stride)


def _attn_kernel(q_ref, k_ref, v_ref, o_ref, num_ref, den_ref, mx_ref, q4_ref, k4_ref, v4_ref,
                 band_ref, first_ref):
    seq = q_ref.shape[2]
    slab_rows = seq // 4
    lane = lax.broadcasted_iota(jnp.int32, (1, LANES), 1)
    head_sel = (lane < HEAD_DIM, lane >= HEAD_DIM)
    nt = (((1,), (1,)), ((), ()))

    def ones_sel(h, rows):
        keep = lax.broadcasted_iota(jnp.int32, (rows, LANES), 1) < HEAD_DIM
        return jnp.where(keep if h == 0 else ~keep, 1.0, 0.0).astype(bf16)

    dist = (SPAN + lax.broadcasted_iota(jnp.int32, (SPAN, 2 * SPAN), 0)
            - lax.broadcasted_iota(jnp.int32, (SPAN, 2 * SPAN), 1))
    band_ref[...] = jnp.where((dist >= 0) & (dist <= SPAN), 0.0, -jnp.inf)
    first_ref[...] = jnp.where(lax.broadcasted_iota(jnp.int32, (SPAN, SPAN), 0)
                               >= lax.broadcasted_iota(jnp.int32, (SPAN, SPAN), 1), 0.0, -jnp.inf)

    def make_units(qall, kall, vall, count, has_first, sinks):
        qall = (qall * Q_SCALE).astype(bf16)
        kall, vall = kall.astype(bf16), vall.astype(bf16)
        units = []
        for u in range(count):
            q = qall[u * SPAN:(u + 1) * SPAN]
            if has_first and u == 0:
                k, v, bias = kall[:SPAN], vall[:SPAN], first_ref
            else:
                i = u if has_first else u + 1
                k, v = kall[(i - 1) * SPAN:(i + 1) * SPAN], vall[(i - 1) * SPAN:(i + 1) * SPAN]
                bias = band_ref
            units.append((q, k, v, bias, sinks[u]))
        return units

    def run(units):
        scores = {}

        def issue_qk(i):
            q, k, _, bias, _ = units[i]
            scores[i] = [lax.dot_general(q, jnp.where(sel, k, jnp.zeros_like(k)), nt,
                                         preferred_element_type=f32) + bias[...]
                         for sel in head_sel]

        def finish(i):
            _, _, v, _, (p, slab, rows) = units[i]
            probs, maxes = [], []
            for s in scores.pop(i):
                mx = jnp.max(s, axis=-1, keepdims=True)
                probs.append(jnp.exp2(s - mx).astype(bf16))
                maxes.append(mx)
            vaug = jnp.concatenate(
                [jnp.concatenate([jnp.where(sel, v, jnp.zeros_like(v)), ones_sel(h, v.shape[0])],
                                 axis=1) for h, sel in enumerate(head_sel)], axis=0)
            acc = jnp.dot(jnp.concatenate(probs, axis=1), vaug, preferred_element_type=f32)
            num_ref[p, slab, rows, :] = acc[:, :LANES]
            den_ref[p, slab, rows, :] = acc[:, LANES:]
            mx_ref[p, slab, rows, :] = jnp.where(head_sel[0], maxes[0], maxes[1])

        for i in range(min(QK_LOOKAHEAD, len(units))):
            issue_qk(i)
        for i in range(len(units)):
            if i + QK_LOOKAHEAD < len(units):
                issue_qk(i + QK_LOOKAHEAD)
            finish(i)

    def d1_group(n0, has_first):
        lo = n0 if has_first else n0 - 1
        nk = ATTN_GROUP if has_first else ATTN_GROUP + 1
        sinks = []
        for u in range(ATTN_GROUP):
            start = (n0 + u) * SPAN
            sinks.append((0, start // slab_rows, _rows(start % slab_rows, SPAN)))
        run(make_units(q_ref[0, 0, _rows(n0 * SPAN, ATTN_GROUP * SPAN), :],
                       k_ref[0, 0, _rows(lo * SPAN, nk * SPAN), :],
                       v_ref[0, 0, _rows(lo * SPAN, nk * SPAN), :],
                       ATTN_GROUP, has_first, sinks))

    d1_group(0, True)

    def d1_step(g, carry):
        d1_group(g * ATTN_GROUP, False)
        return carry

    lax.fori_loop(1, seq // (SPAN * ATTN_GROUP), d1_step, 0)

    blocks4 = slab_rows // SPAN
    classes4 = ATTN_GROUP // blocks4

    def d4_step(g, carry):
        units = []
        for cc in range(classes4):
            c = g * classes4 + cc
            qall = q_ref[0, 0, _rows(c, slab_rows, 4), :]
            kall = k_ref[0, 0, _rows(c, slab_rows, 4), :]
            vall = v_ref[0, 0, _rows(c, slab_rows, 4), :]
            q4_ref[c], k4_ref[c], v4_ref[c] = qall, kall, vall
            sinks = [(1, c, _rows(u * SPAN, SPAN)) for u in range(blocks4)]
            units += make_units(qall, kall, vall, blocks4, True, sinks)
        run(units)
        return carry

    lax.fori_loop(0, 4 // classes4, d4_step, 0)

    blocks16 = slab_rows // (4 * SPAN)
    classes16 = ATTN_GROUP // blocks16

    def d16_step(g, carry):
        units = []
        for cc in range(classes16):
            c16 = g * classes16 + cc
            c4, j = c16 % 4, c16 // 4
            rows = _rows(j, blocks16 * SPAN, 4)
            sinks = [(2, c4, _rows(j + 4 * SPAN * u, SPAN, 4)) for u in range(blocks16)]
            units += make_units(q4_ref[c4, rows, :], k4_ref[c4, rows, :], v4_ref[c4, rows, :],
                                blocks16, True, sinks)
        run(units)
        return carry

    lax.fori_loop(0, MAX_DILATION // classes16, d16_step, 0)

    merge_rows = 256
    per_slab = slab_rows // merge_rows

    def merge(i, carry):
        c4 = i // per_slab
        m0 = (i - c4 * per_slab) * merge_rows
        rows = _rows(m0, merge_rows)
        tok = c4 + 4 * m0
        trows = _rows(tok % slab_rows, merge_rows, 4)
        tslab = tok // slab_rows

        def stat(ref, p):
            return ref[p, tslab, trows, :] if p == 0 else ref[p, c4, rows, :]

        mx = [stat(mx_ref, p) for p in range(3)]
        top = jnp.maximum(jnp.maximum(mx[0], mx[1]), mx[2])
        w = [jnp.exp2(m - top) for m in mx]
        num = sum(w[p] * stat(num_ref, p) for p in range(3))
        den = sum(w[p] * stat(den_ref, p) for p in range(3))
        o_ref[0, _rows(tok, merge_rows, 4), :] = num / den
        return carry

    lax.fori_loop(0, 4 * per_slab, merge, 0)


def _attention(qkv):
    b, _, s, _ = qkv.shape
    assert s % (MAX_DILATION * SPAN) == 0 and (s // SPAN) % ATTN_GROUP == 0
    assert ATTN_GROUP % (s // (4 * SPAN)) == 0 and 4 % (ATTN_GROUP // (s // (4 * SPAN))) == 0
    blk = (1, 1, s, LANES)
    stat = pltpu.VMEM((len(DILATIONS), 4, s // 4, LANES), f32)
    mod4 = pltpu.VMEM((4, s // 4, LANES), f32)
    return pl.pallas_call(
        _attn_kernel,
        grid=(b, HEAD_PAIRS),
        in_specs=[
            pl.BlockSpec(blk, lambda i, j: (i, j, 0, 0)),
            pl.BlockSpec(blk, lambda i, j: (i, HEAD_PAIRS + j, 0, 0)),
            pl.BlockSpec(blk, lambda i, j: (i, 2 * HEAD_PAIRS + j, 0, 0)),
        ],
        out_specs=pl.BlockSpec((1, s, LANES), lambda i, j: (i, 0, j)),
        out_shape=jax.ShapeDtypeStruct((b, s, ATTN_WIDTH), f32),
        scratch_shapes=[
            stat, stat, stat, mod4, mod4, mod4,
            pltpu.VMEM((SPAN, 2 * SPAN), f32),
            pltpu.VMEM((SPAN, SPAN), f32),
        ],
        compiler_params=pltpu.CompilerParams(
            dimension_semantics=("parallel", "parallel"), vmem_limit_bytes=VMEM_LIMIT),
        name="dilated_attention",
    )(qkv, qkv, qkv)


def _sigmoid_of_twice(half_x):
    return 0.5 * jnp.tanh(half_x) + 0.5


def _rglru_kernel(xr_ref, gr_ref, cw_ref, cb_ref, wg_ref, br_ref, bi_ref, lam_ref,
                  o_ref, xi_ref, a_ref, u_ref, h_ref, *, batch):
    tc = LRU_TIME_CHUNK
    halo = (CONV_WIDTH - 1) * batch
    step = pl.program_id(0)

    @pl.when(step == 0)
    def _():
        h_ref[...] = jnp.zeros_like(h_ref)
        xi_ref[:, 0:halo, :] = jnp.zeros((LRU_GROUPS, halo, LANES), f32)

    def interleave(b, carry):
        x = xr_ref[b]
        for g in range(LRU_GROUPS):
            xi_ref[g, pl.ds(halo + b, tc, stride=batch), :] = x[:, g * LANES:(g + 1) * LANES]
        return carry

    lax.fori_loop(0, batch, interleave, 0)

    neg_lam = -lam_ref[...]
    softplus = jnp.maximum(neg_lam, 0.0) + jnp.log1p(jnp.exp(-jnp.abs(neg_lam)))
    rate = -LRU_C * softplus
    half = LRU_WIDTH // 2
    chunk = LRU_DENSE_ROWS

    def dense(ci, carry):
        r0 = pl.multiple_of(ci * chunk, chunk)
        y = cb_ref[...]
        for j in range(CONV_WIDTH):
            tap = CONV_WIDTH - 1 - j
            rows = pl.ds(pl.multiple_of(r0 + halo - j * batch, batch), chunk)
            xs = jnp.concatenate([xi_ref[g, rows, :] for g in range(LRU_GROUPS)], axis=-1)
            y = y + cw_ref[tap:tap + 1, :] * xs
        yb = y.astype(bf16)
        gates = [jnp.dot(yb[:, hf * half:(hf + 1) * half], wg_ref[hf],
                         preferred_element_type=f32) for hf in range(2)]
        pre_r = jnp.concatenate([gates[0][:, :half], gates[1][:, :half]], axis=-1)
        pre_i = jnp.concatenate([gates[0][:, half:], gates[1][:, half:]], axis=-1)
        r = _sigmoid_of_twice(pre_r + br_ref[...])
        i = _sigmoid_of_twice(pre_i + bi_ref[...])
        log_a = rate * r
        a = jnp.exp(log_a)
        gain_sq = -jnp.tanh(log_a) * (a * a + 1.0)
        u = (gain_sq * lax.rsqrt(jnp.maximum(gain_sq, SQRT_FLOOR))) * (i * y)
        for g in range(LRU_GROUPS):
            cols = slice(g * LANES, (g + 1) * LANES)
            a_ref[g, pl.ds(r0, chunk), :] = a[:, cols]
            u_ref[g, pl.ds(r0, chunk), :] = u[:, cols]
        return carry

    lax.fori_loop(0, tc * batch // chunk, dense, 0)
    xi_ref[:, 0:halo, :] = xi_ref[:, tc * batch:tc * batch + halo, :]

    def scan(t, hs):
        rows = pl.ds(pl.multiple_of(t * batch, batch), batch)
        new = []
        for g in range(LRU_GROUPS):
            h = a_ref[g, rows, :] * hs[g] + u_ref[g, rows, :]
            u_ref[g, rows, :] = h
            new.append(h)
        return tuple(new)

    hs = lax.fori_loop(0, tc, scan, tuple(h_ref[g] for g in range(LRU_GROUPS)), unroll=8)
    for g in range(LRU_GROUPS):
        h_ref[g] = hs[g]

    def emit(b, carry):
        h = jnp.concatenate(
            [u_ref[g, pl.ds(b, tc, stride=batch), :] for g in range(LRU_GROUPS)], axis=-1)
        o_ref[b] = (h * jax.nn.gelu(gr_ref[b])).astype(o_ref.dtype)
        return carry

    lax.fori_loop(0, batch, emit, 0)


def _rglru(zr3, conv_w, conv_b, w_gates, b_r, b_i, lam):
    b, s, _ = zr3.shape
    assert b == SUBLANES
    tc = LRU_TIME_CHUNK
    blk = (b, tc, LRU_WIDTH)
    vec = pl.BlockSpec((1, LRU_WIDTH), lambda i: (0, 0))
    return pl.pallas_call(
        functools.partial(_rglru_kernel, batch=b),
        grid=(s // tc,),
        in_specs=[
            pl.BlockSpec(blk, lambda i: (0, i, 0)),
            pl.BlockSpec(blk, lambda i: (0, i, 1)),
            pl.BlockSpec((CONV_WIDTH, LRU_WIDTH), lambda i: (0, 0)),
            vec,
            pl.BlockSpec((2, LRU_WIDTH // 2, LRU_WIDTH), lambda i: (0, 0, 0)),
            vec, vec, vec,
        ],
        out_specs=pl.BlockSpec(blk, lambda i: (0, i, 0)),
        out_shape=jax.ShapeDtypeStruct((b, s, LRU_WIDTH), bf16),
        scratch_shapes=[
            pltpu.VMEM((LRU_GROUPS, (tc + CONV_WIDTH - 1) * b, LANES), f32),
            pltpu.VMEM((LRU_GROUPS, tc * b, LANES), f32),
            pltpu.VMEM((LRU_GROUPS, tc * b, LANES), f32),
            pltpu.VMEM((LRU_GROUPS, b, LANES), f32),
        ],
        compiler_params=pltpu.CompilerParams(
            dimension_semantics=("arbitrary",), vmem_limit_bytes=VMEM_LIMIT),
        name="rglru",
    )(zr3, zr3, conv_w, conv_b, w_gates, b_r, b_i, lam)


def _gate_weights(w_r, w_i):
    def half_dense(w, hf):
        blocks = w[hf * 4:(hf + 1) * 4]
        return jax.scipy.linalg.block_diag(*[blocks[n] for n in range(4)])
    return (jnp.stack([
        jnp.concatenate([half_dense(w_r, hf), half_dense(w_i, hf)], axis=-1) for hf in range(2)
    ]) * GATE_PRESCALE).astype(bf16)


def _out_mlp_kernel(attn_ref, rec_ref, x_ref, gpost_ref, gpre2_ref, gpost2_ref,
                    wo_ref, w1_ref, w2_ref, o_ref):
    halves = [slice(i * ROW_TILE // 2, (i + 1) * ROW_TILE // 2) for i in range(2)]
    ff_chunk = 2 * N_CHUNK
    mix = [jnp.concatenate([attn_ref[r, :].astype(bf16), rec_ref[r, :]], axis=-1)
           for r in halves]
    m = [jnp.dot(mx, wo_ref[...], preferred_element_type=f32) for mx in mix]
    x1, y = [], []
    for i, r in enumerate(halves):
        x1.append(x_ref[r, :] + _rms(m[i], gpost_ref[...]))
        h = _rms(x1[i], gpre2_ref[...]).astype(bf16)
        acc = None
        for n in range(D_FF // ff_chunk):
            cols = slice(n * ff_chunk, (n + 1) * ff_chunk)
            a = jnp.maximum(jnp.dot(h, w1_ref[:, cols], preferred_element_type=f32), 0.0)
            part = jnp.dot((a * a).astype(bf16), w2_ref[cols, :], preferred_element_type=f32)
            acc = part if acc is None else acc + part
        y.append(acc)
    for i, r in enumerate(halves):
        o_ref[r, :] = x1[i] + _rms(y[i], gpost2_ref[...])


def _out_mlp(attn2d, rec2d, x2d, g_post, g_pre2, g_post2, wo, w1, w2):
    t = x2d.shape[0]
    vec = pl.BlockSpec((1, D_MODEL), lambda i: (0, 0))
    const = lambda shape: pl.BlockSpec(shape, lambda i: (0, 0), pipeline_mode=pl.Buffered(1))
    return pl.pallas_call(
        _out_mlp_kernel,
        grid=(t // ROW_TILE,),
        in_specs=[
            pl.BlockSpec((ROW_TILE, ATTN_WIDTH), lambda i: (i, 0)),
            pl.BlockSpec((ROW_TILE, LRU_WIDTH), lambda i: (i, 0)),
            pl.BlockSpec((ROW_TILE, D_MODEL), lambda i: (i, 0)),
            vec, vec, vec,
            const((ATTN_WIDTH + LRU_WIDTH, D_MODEL)),
            const((D_MODEL, D_FF)),
            const((D_FF, D_MODEL)),
        ],
        out_specs=pl.BlockSpec((ROW_TILE, D_MODEL), lambda i: (i, 0)),
        out_shape=jax.ShapeDtypeStruct((t, D_MODEL), f32),
        compiler_params=pltpu.CompilerParams(
            dimension_semantics=("parallel",), vmem_limit_bytes=VMEM_LIMIT),
        name="out_mlp",
    )(attn2d, rec2d, x2d, g_post, g_pre2, g_post2, wo, w1, w2)


def kernel(x, mix_norm_pre, mix_norm_post, mlp_norm_pre, mlp_norm_post, w_in, conv_w, conv_b,
           w_rgate, b_rgate, w_igate, b_igate, lru_lambda, w_out, w_ff_in, w_ff_out):
    b, s, d = x.shape
    depth = w_in.shape[0]
    t = b * s
    x2d = x.reshape(t, d)
    row = lambda v: v.reshape(1, -1)
    for l in range(depth):
        qkv, zr = _in_proj(x2d, row(mix_norm_pre[l]), w_in[l].astype(bf16), b, s)
        attn = _attention(qkv)
        rec = _rglru(zr.reshape(b, s, 2 * LRU_WIDTH), conv_w[l], row(conv_b[l]),
                     _gate_weights(w_rgate[l], w_igate[l]),
                     row(b_rgate[l]) * GATE_PRESCALE, row(b_igate[l]) * GATE_PRESCALE,
                     row(lru_lambda[l]))
        x2d = _out_mlp(attn.reshape(t, ATTN_WIDTH), rec.reshape(t, LRU_WIDTH), x2d,
                       row(mix_norm_post[l]), row(mlp_norm_pre[l]), row(mlp_norm_post[l]),
                       w_out[l].astype(bf16), w_ff_in[l].astype(bf16), w_ff_out[l].astype(bf16))
    return x2d.reshape(b, s, d)
```

```python
import functools
import math

import jax
import jax.numpy as jnp
from jax import lax
from jax.experimental import pallas as pl
from jax.experimental.pallas import tpu as pltpu

D_MODEL = 1024
N_HEADS = 8
HEAD_DIM = 64
ATTN_WIDTH = N_HEADS * HEAD_DIM
DILATIONS = (1, 4, 16)
MAX_DILATION = 16
SPAN = 128
LRU_WIDTH = 512
N_LRU_BLOCKS = 8
LRU_BLOCK = LRU_WIDTH // N_LRU_BLOCKS
CONV_WIDTH = 4
LRU_C = 8.0
IN_WIDTH = 3 * ATTN_WIDTH + 2 * LRU_WIDTH
D_FF = 4 * D_MODEL
NORM_EPS = 1e-6

LANES = 128
SUBLANES = 8
VMEM_LIMIT = 56 * 1024 * 1024

ROW_TILE = 512
IN_ROW_TILE = 1024
N_CHUNK = 512
LRU_TIME_CHUNK = 256
LRU_DENSE_ROWS = 256
HEAD_PAIRS = ATTN_WIDTH // LANES
QKV_SLABS = 3 * HEAD_PAIRS
LRU_GROUPS = LRU_WIDTH // LANES
ATTN_GROUP = 32
QK_LOOKAHEAD = 4
GATE_PRESCALE = 0.5
SQRT_FLOOR = 1e-37
Q_SCALE = HEAD_DIM ** -0.5 * math.log2(math.e)

bf16 = jnp.bfloat16
f32 = jnp.float32


def _rms(x, g):
    y = x * lax.rsqrt(jnp.mean(x * x, axis=-1, keepdims=True) + NORM_EPS)
    return y * g


def _layer_vec(width, layer):
    return pl.BlockSpec((None, 1, width), lambda i: (layer, 0, 0))


def _in_proj_kernel(x_ref, g_ref, w_ref, wo_ref, w1_ref, w2_ref,
                    qkv_ref, zr_ref, wo_out_ref, w1_out_ref, w2_out_ref):
    h = _rms(x_ref[...], g_ref[...]).astype(bf16)
    slabs_per_chunk = N_CHUNK // LANES
    qkv_chunks = QKV_SLABS // slabs_per_chunk
    for n in range(IN_WIDTH // N_CHUNK):
        z = jnp.dot(h, w_ref[:, n * N_CHUNK:(n + 1) * N_CHUNK], preferred_element_type=f32)
        if n < qkv_chunks:
            for s in range(slabs_per_chunk):
                qkv_ref[0, n * slabs_per_chunk + s] = z[:, s * LANES:(s + 1) * LANES]
        else:
            m = n - qkv_chunks
            zr_ref[:, m * N_CHUNK:(m + 1) * N_CHUNK] = z
    wo_out_ref[...] = wo_ref[...].astype(bf16)
    w1_out_ref[...] = w1_ref[...].astype(bf16)
    w2_out_ref[...] = w2_ref[...].astype(bf16)


def _in_proj(x2d, gains, w_in_bf16, w_out, w_ff_in, w_ff_out, layer, batch, seq):
    t = x2d.shape[0]
    steps = t // IN_ROW_TILE
    tiles_per_seq = seq // IN_ROW_TILE
    rows_in = lambda w: pl.BlockSpec((None, w.shape[1] // steps, w.shape[2]),
                                     lambda i: (layer, i, 0))
    rows_out = lambda w: pl.BlockSpec((w.shape[1] // steps, w.shape[2]), lambda i: (i, 0))
    cast = (w_out, w_ff_in, w_ff_out)
    for w in cast:
        assert w.shape[1] % (steps * 2 * SUBLANES) == 0
    return pl.pallas_call(
        _in_proj_kernel,
        grid=(steps,),
        in_specs=[
            pl.BlockSpec((IN_ROW_TILE, D_MODEL), lambda i: (i, 0)),
            _layer_vec(D_MODEL, layer),
            pl.BlockSpec((None, D_MODEL, IN_WIDTH), lambda i: (layer, 0, 0),
                         pipeline_mode=pl.Buffered(1)),
        ] + [rows_in(w) for w in cast],
        out_specs=[
            pl.BlockSpec((1, QKV_SLABS, IN_ROW_TILE, LANES),
                         lambda i: (i // tiles_per_seq, 0, i % tiles_per_seq, 0)),
            pl.BlockSpec((IN_ROW_TILE, 2 * LRU_WIDTH), lambda i: (i, 0)),
        ] + [rows_out(w) for w in cast],
        out_shape=[
            jax.ShapeDtypeStruct((batch, QKV_SLABS, seq, LANES), f32),
            jax.ShapeDtypeStruct((t, 2 * LRU_WIDTH), f32),
        ] + [jax.ShapeDtypeStruct(w.shape[1:], bf16) for w in cast],
        compiler_params=pltpu.CompilerParams(
            dimension_semantics=("parallel",), vmem_limit_bytes=VMEM_LIMIT),
        name="in_proj",
    )(x2d, gains, w_in_bf16, *cast)


def _rows(start, size, stride=1):
    if stride == 1:
        if not isinstance(start, int):
            start = pl.multiple_of(start, SPAN)
        return pl.ds(start, size)
    return pl.ds(start, size, stride=stride)


def _attn_kernel(q_ref, k_ref, v_ref, o_ref, num_ref, den_ref, mx_ref, q4_ref, k4_ref, v4_ref,
                 band_ref, first_ref):
    seq = q_ref.shape[2]
    slab_rows = seq // 4
    lane = lax.broadcasted_iota(jnp.int32, (1, LANES), 1)
    head_sel = (lane < HEAD_DIM, lane >= HEAD_DIM)
    nt = (((1,), (1,)), ((), ()))

    def ones_sel(h, rows):
        keep = lax.broadcasted_iota(jnp.int32, (rows, LANES), 1) < HEAD_DIM
        return jnp.where(keep if h == 0 else ~keep, 1.0, 0.0).astype(bf16)

    dist = (SPAN + lax.broadcasted_iota(jnp.int32, (SPAN, 2 * SPAN), 0)
            - lax.broadcasted_iota(jnp.int32, (SPAN, 2 * SPAN), 1))
    band_ref[...] = jnp.where((dist >= 0) & (dist <= SPAN), 0.0, -jnp.inf)
    first_ref[...] = jnp.where(lax.broadcasted_iota(jnp.int32, (SPAN, SPAN), 0)
                               >= lax.broadcasted_iota(jnp.int32, (SPAN, SPAN), 1), 0.0, -jnp.inf)

    def make_units(qall, kall, vall, count, has_first, sinks):
        qall = (qall * Q_SCALE).astype(bf16)
        kall, vall = kall.astype(bf16), vall.astype(bf16)
        units = []
        for u in range(count):
            q = qall[u * SPAN:(u + 1) * SPAN]
            if has_first and u == 0:
                k, v, bias = kall[:SPAN], vall[:SPAN], first_ref
            else:
                i = u if has_first else u + 1
                k, v = kall[(i - 1) * SPAN:(i + 1) * SPAN], vall[(i - 1) * SPAN:(i + 1) * SPAN]
                bias = band_ref
            units.append((q, k, v, bias, sinks[u]))
        return units

    def run(units):
        scores = {}

        def issue_qk(i):
            q, k, _, bias, _ = units[i]
            scores[i] = [lax.dot_general(q, jnp.where(sel, k, jnp.zeros_like(k)), nt,
                                         preferred_element_type=f32) + bias[...]
                         for sel in head_sel]

        def finish(i):
            _, _, v, _, (p, slab, rows) = units[i]
            probs, maxes = [], []
            for s in scores.pop(i):
                mx = jnp.max(s, axis=-1, keepdims=True)
                probs.append(jnp.exp2(s - mx).astype(bf16))
                maxes.append(mx)
            vaug = jnp.concatenate(
                [jnp.concatenate([jnp.where(sel, v, jnp.zeros_like(v)), ones_sel(h, v.shape[0])],
                                 axis=1) for h, sel in enumerate(head_sel)], axis=0)
            acc = jnp.dot(jnp.concatenate(probs, axis=1), vaug, preferred_element_type=f32)
            num_ref[p, slab, rows, :] = acc[:, :LANES]
            den_ref[p, slab, rows, :] = acc[:, LANES:]
            mx_ref[p, slab, rows, :] = jnp.where(head_sel[0], maxes[0], maxes[1])

        for i in range(min(QK_LOOKAHEAD, len(units))):
            issue_qk(i)
        for i in range(len(units)):
            if i + QK_LOOKAHEAD < len(units):
                issue_qk(i + QK_LOOKAHEAD)
            finish(i)

    def d1_group(n0, has_first):
        lo = n0 if has_first else n0 - 1
        nk = ATTN_GROUP if has_first else ATTN_GROUP + 1
        sinks = []
        for u in range(ATTN_GROUP):
            start = (n0 + u) * SPAN
            sinks.append((0, start // slab_rows, _rows(start % slab_rows, SPAN)))
        run(make_units(q_ref[0, 0, _rows(n0 * SPAN, ATTN_GROUP * SPAN), :],
                       k_ref[0, 0, _rows(lo * SPAN, nk * SPAN), :],
                       v_ref[0, 0, _rows(lo * SPAN, nk * SPAN), :],
                       ATTN_GROUP, has_first, sinks))

    d1_group(0, True)

    def d1_step(g, carry):
        d1_group(g * ATTN_GROUP, False)
        return carry

    lax.fori_loop(1, seq // (SPAN * ATTN_GROUP), d1_step, 0)

    blocks4 = slab_rows // SPAN
    classes4 = ATTN_GROUP // blocks4

    def d4_step(g, carry):
        units = []
        for cc in range(classes4):
            c = g * classes4 + cc
            qall = q_ref[0, 0, _rows(c, slab_rows, 4), :]
            kall = k_ref[0, 0, _rows(c, slab_rows, 4), :]
            vall = v_ref[0, 0, _rows(c, slab_rows, 4), :]
            q4_ref[c], k4_ref[c], v4_ref[c] = qall, kall, vall
            sinks = [(1, c, _rows(u * SPAN, SPAN)) for u in range(blocks4)]
            units += make_units(qall, kall, vall, blocks4, True, sinks)
        run(units)
        return carry

    lax.fori_loop(0, 4 // classes4, d4_step, 0)

    blocks16 = slab_rows // (4 * SPAN)
    classes16 = ATTN_GROUP // blocks16

    def d16_step(g, carry):
        units = []
        for cc in range(classes16):
            c16 = g * classes16 + cc
            c4, j = c16 % 4, c16 // 4
            rows = _rows(j, blocks16 * SPAN, 4)
            sinks = [(2, c4, _rows(j + 4 * SPAN * u, SPAN, 4)) for u in range(blocks16)]
            units += make_units(q4_ref[c4, rows, :], k4_ref[c4, rows, :], v4_ref[c4, rows, :],
                                blocks16, True, sinks)
        run(units)
        return carry

    lax.fori_loop(0, MAX_DILATION // classes16, d16_step, 0)

    merge_rows = 256
    per_slab = slab_rows // merge_rows

    def merge(i, carry):
        c4 = i // per_slab
        m0 = (i - c4 * per_slab) * merge_rows
        rows = _rows(m0, merge_rows)
        tok = c4 + 4 * m0
        trows = _rows(tok % slab_rows, merge_rows, 4)
        tslab = tok // slab_rows

        def stat(ref, p):
            return ref[p, tslab, trows, :] if p == 0 else ref[p, c4, rows, :]

        mx = [stat(mx_ref, p) for p in range(3)]
        top = jnp.maximum(jnp.maximum(mx[0], mx[1]), mx[2])
        w = [jnp.exp2(m - top) for m in mx]
        num = sum(w[p] * stat(num_ref, p) for p in range(3))
        den = sum(w[p] * stat(den_ref, p) for p in range(3))
        o_ref[0, _rows(tok, merge_rows, 4), :] = num / den
        return carry

    lax.fori_loop(0, 4 * per_slab, merge, 0)


def _attention(qkv):
    b, _, s, _ = qkv.shape
    assert s % (MAX_DILATION * SPAN) == 0 and (s // SPAN) % ATTN_GROUP == 0
    assert ATTN_GROUP % (s // (4 * SPAN)) == 0 and 4 % (ATTN_GROUP // (s // (4 * SPAN))) == 0
    blk = (1, 1, s, LANES)
    stat = pltpu.VMEM((len(DILATIONS), 4, s // 4, LANES), f32)
    mod4 = pltpu.VMEM((4, s // 4, LANES), f32)
    return pl.pallas_call(
        _attn_kernel,
        grid=(b, HEAD_PAIRS),
        in_specs=[
            pl.BlockSpec(blk, lambda i, j: (i, j, 0, 0)),
            pl.BlockSpec(blk, lambda i, j: (i, HEAD_PAIRS + j, 0, 0)),
            pl.BlockSpec(blk, lambda i, j: (i, 2 * HEAD_PAIRS + j, 0, 0)),
        ],
        out_specs=pl.BlockSpec((1, s, LANES), lambda i, j: (i, 0, j)),
        out_shape=jax.ShapeDtypeStruct((b, s, ATTN_WIDTH), f32),
        scratch_shapes=[
            stat, stat, stat, mod4, mod4, mod4,
            pltpu.VMEM((SPAN, 2 * SPAN), f32),
            pltpu.VMEM((SPAN, SPAN), f32),
        ],
        compiler_params=pltpu.CompilerParams(
            dimension_semantics=("parallel", "parallel"), vmem_limit_bytes=VMEM_LIMIT),
        name="dilated_attention",
    )(qkv, qkv, qkv)


def _sigmoid_of_twice(half_x):
    return 0.5 * jnp.tanh(half_x) + 0.5


def _rglru_kernel(xr_ref, gr_ref, cw_ref, cb_ref, wg_ref, br_ref, bi_ref, lam_ref,
                  o_ref, xi_ref, a_ref, u_ref, h_ref, *, batch):
    tc = LRU_TIME_CHUNK
    halo = (CONV_WIDTH - 1) * batch
    step = pl.program_id(0)

    @pl.when(step == 0)
    def _():
        h_ref[...] = jnp.zeros_like(h_ref)
        xi_ref[:, 0:halo, :] = jnp.zeros((LRU_GROUPS, halo, LANES), f32)

    def interleave(b, carry):
        x = xr_ref[b]
        for g in range(LRU_GROUPS):
            xi_ref[g, pl.ds(halo + b, tc, stride=batch), :] = x[:, g * LANES:(g + 1) * LANES]
        return carry

    lax.fori_loop(0, batch, interleave, 0)

    neg_lam = -lam_ref[...]
    softplus = jnp.maximum(neg_lam, 0.0) + jnp.log1p(jnp.exp(-jnp.abs(neg_lam)))
    rate = -LRU_C * softplus
    half = LRU_WIDTH // 2
    chunk = LRU_DENSE_ROWS

    def dense(ci, carry):
        r0 = pl.multiple_of(ci * chunk, chunk)
        y = cb_ref[...]
        for j in range(CONV_WIDTH):
            tap = CONV_WIDTH - 1 - j
            rows = pl.ds(pl.multiple_of(r0 + halo - j * batch, batch), chunk)
            xs = jnp.concatenate([xi_ref[g, rows, :] for g in range(LRU_GROUPS)], axis=-1)
            y = y + cw_ref[tap:tap + 1, :] * xs
        yb = y.astype(bf16)
        gates = [jnp.dot(yb[:, hf * half:(hf + 1) * half], wg_ref[hf],
                         preferred_element_type=f32) for hf in range(2)]
        pre_r = jnp.concatenate([gates[0][:, :half], gates[1][:, :half]], axis=-1)
        pre_i = jnp.concatenate([gates[0][:, half:], gates[1][:, half:]], axis=-1)
        r = _sigmoid_of_twice(pre_r + br_ref[...])
        i = _sigmoid_of_twice(pre_i + bi_ref[...])
        log_a = rate * r
        a = jnp.exp(log_a)
        gain_sq = -jnp.tanh(log_a) * (a * a + 1.0)
        u = (gain_sq * lax.rsqrt(jnp.maximum(gain_sq, SQRT_FLOOR))) * (i * y)
        for g in range(LRU_GROUPS):
            cols = slice(g * LANES, (g + 1) * LANES)
            a_ref[g, pl.ds(r0, chunk), :] = a[:, cols]
            u_ref[g, pl.ds(r0, chunk), :] = u[:, cols]
        return carry

    lax.fori_loop(0, tc * batch // chunk, dense, 0)
    xi_ref[:, 0:halo, :] = xi_ref[:, tc * batch:tc * batch + halo, :]

    def scan(t, hs):
        rows = pl.ds(pl.multiple_of(t * batch, batch), batch)
        new = []
        for g in range(LRU_GROUPS):
            h = a_ref[g, rows, :] * hs[g] + u_ref[g, rows, :]
            u_ref[g, rows, :] = h
            new.append(h)
        return tuple(new)

    hs = lax.fori_loop(0, tc, scan, tuple(h_ref[g] for g in range(LRU_GROUPS)), unroll=8)
    for g in range(LRU_GROUPS):
        h_ref[g] = hs[g]

    def emit(b, carry):
        h = jnp.concatenate(
            [u_ref[g, pl.ds(b, tc, stride=batch), :] for g in range(LRU_GROUPS)], axis=-1)
        o_ref[b] = (h * jax.nn.gelu(gr_ref[b])).astype(o_ref.dtype)
        return carry

    lax.fori_loop(0, batch, emit, 0)


def _rglru(zr3, conv_w, conv_b, w_gates, b_r, b_i, lam, layer):
    b, s, _ = zr3.shape
    assert b == SUBLANES
    tc = LRU_TIME_CHUNK
    blk = (b, tc, LRU_WIDTH)
    vec = _layer_vec(LRU_WIDTH, layer)
    return pl.pallas_call(
        functools.partial(_rglru_kernel, batch=b),
        grid=(s // tc,),
        in_specs=[
            pl.BlockSpec(blk, lambda i: (0, i, 0)),
            pl.BlockSpec(blk, lambda i: (0, i, 1)),
            pl.BlockSpec((None, CONV_WIDTH, LRU_WIDTH), lambda i: (layer, 0, 0)),
            vec,
            pl.BlockSpec((None, 2, LRU_WIDTH // 2, LRU_WIDTH), lambda i: (layer, 0, 0, 0)),
            vec, vec, vec,
        ],
        out_specs=pl.BlockSpec(blk, lambda i: (0, i, 0)),
        out_shape=jax.ShapeDtypeStruct((b, s, LRU_WIDTH), bf16),
        scratch_shapes=[
            pltpu.VMEM((LRU_GROUPS, (tc + CONV_WIDTH - 1) * b, LANES), f32),
            pltpu.VMEM((LRU_GROUPS, tc * b, LANES), f32),
            pltpu.VMEM((LRU_GROUPS, tc * b, LANES), f32),
            pltpu.VMEM((LRU_GROUPS, b, LANES), f32),
        ],
        compiler_params=pltpu.CompilerParams(
            dimension_semantics=("arbitrary",), vmem_limit_bytes=VMEM_LIMIT),
        name="rglru",
    )(zr3, zr3, conv_w, conv_b, w_gates, b_r, b_i, lam)


def _gate_weights(w_r, w_i):
    per_half = N_LRU_BLOCKS // 2
    eye = jnp.eye(per_half, dtype=w_r.dtype)

    def dense(w):
        d = w.shape[0]
        blocks = w.reshape(d, 2, per_half, LRU_BLOCK, LRU_BLOCK)
        full = jnp.einsum('dhaij,ab->dhaibj', blocks, eye)
        return full.reshape(d, 2, per_half * LRU_BLOCK, per_half * LRU_BLOCK)

    return (jnp.concatenate([dense(w_r), dense(w_i)], axis=-1) * GATE_PRESCALE).astype(bf16)


def _out_mlp_kernel(attn_ref, rec_ref, x_ref, gpost_ref, gpre2_ref, gpost2_ref,
                    wo_ref, w1_ref, w2_ref, o_ref):
    halves = [slice(i * ROW_TILE // 2, (i + 1) * ROW_TILE // 2) for i in range(2)]
    ff_chunk = 2 * N_CHUNK
    mix = [jnp.concatenate([attn_ref[r, :].astype(bf16), rec_ref[r, :]], axis=-1)
           for r in halves]
    m = [jnp.dot(mx, wo_ref[...], preferred_element_type=f32) for mx in mix]
    x1, y = [], []
    for i, r in enumerate(halves):
        x1.append(x_ref[r, :] + _rms(m[i], gpost_ref[...]))
        h = _rms(x1[i], gpre2_ref[...]).astype(bf16)
        acc = None
        for n in range(D_FF // ff_chunk):
            cols = slice(n * ff_chunk, (n + 1) * ff_chunk)
            a = jnp.maximum(jnp.dot(h, w1_ref[:, cols], preferred_element_type=f32), 0.0)
            part = jnp.dot((a * a).astype(bf16), w2_ref[cols, :], preferred_element_type=f32)
            acc = part if acc is None else acc + part
        y.append(acc)
    for i, r in enumerate(halves):
        o_ref[r, :] = x1[i] + _rms(y[i], gpost2_ref[...])


def _out_mlp(attn2d, rec2d, x2d, g_post, g_pre2, g_post2, wo, w1, w2, layer):
    t = x2d.shape[0]
    vec = _layer_vec(D_MODEL, layer)
    const = lambda shape: pl.BlockSpec(shape, lambda i: (0, 0), pipeline_mode=pl.Buffered(1))
    return pl.pallas_call(
        _out_mlp_kernel,
        grid=(t // ROW_TILE,),
        in_specs=[
            pl.BlockSpec((ROW_TILE, ATTN_WIDTH), lambda i: (i, 0)),
            pl.BlockSpec((ROW_TILE, LRU_WIDTH), lambda i: (i, 0)),
            pl.BlockSpec((ROW_TILE, D_MODEL), lambda i: (i, 0)),
            vec, vec, vec,
            const((ATTN_WIDTH + LRU_WIDTH, D_MODEL)),
            const((D_MODEL, D_FF)),
            const((D_FF, D_MODEL)),
        ],
        out_specs=pl.BlockSpec((ROW_TILE, D_MODEL), lambda i: (i, 0)),
        out_shape=jax.ShapeDtypeStruct((t, D_MODEL), f32),
        compiler_params=pltpu.CompilerParams(
            dimension_semantics=("parallel",), vmem_limit_bytes=VMEM_LIMIT),
        name="out_mlp",
    )(attn2d, rec2d, x2d, g_post, g_pre2, g_post2, wo, w1, w2)


def kernel(x, mix_norm_pre, mix_norm_post, mlp_norm_pre, mlp_norm_post, w_in, conv_w, conv_b,
           w_rgate, b_rgate, w_igate, b_igate, lru_lambda, w_out, w_ff_in, w_ff_out):
    b, s, d = x.shape
    depth = w_in.shape[0]
    t = b * s
    x2d = x.reshape(t, d)
    rows = lambda v: v.reshape(depth, 1, -1)
    w_in_bf16 = w_in.astype(bf16)
    w_gates = _gate_weights(w_rgate, w_igate)
    b_r, b_i = rows(b_rgate) * GATE_PRESCALE, rows(b_igate) * GATE_PRESCALE
    for l in range(depth):
        qkv, zr, wo, w1, w2 = _in_proj(x2d, rows(mix_norm_pre), w_in_bf16, w_out, w_ff_in, w_ff_out,
                                       l, b, s)
        attn = _attention(qkv)
        rec = _rglru(zr.reshape(b, s, 2 * LRU_WIDTH), conv_w, rows(conv_b), w_gates, b_r, b_i,
                     rows(lru_lambda), l)
        x2d = _out_mlp(attn.reshape(t, ATTN_WIDTH), rec.reshape(t, LRU_WIDTH), x2d,
                       rows(mix_norm_post), rows(mlp_norm_pre), rows(mlp_norm_post), wo, w1, w2, l)
    return x2d.reshape(b, s, d)
```

```python
import functools
import math

import jax
import jax.numpy as jnp
from jax import lax
from jax.experimental import pallas as pl
from jax.experimental.pallas import tpu as pltpu

D_MODEL = 1024
N_HEADS = 8
HEAD_DIM = 64
ATTN_WIDTH = N_HEADS * HEAD_DIM
DILATIONS = (1, 4, 16)
MAX_DILATION = 16
SPAN = 128
LRU_WIDTH = 512
N_LRU_BLOCKS = 8
LRU_BLOCK = LRU_WIDTH // N_LRU_BLOCKS
CONV_WIDTH = 4
LRU_C = 8.0
IN_WIDTH = 3 * ATTN_WIDTH + 2 * LRU_WIDTH
D_FF = 4 * D_MODEL
NORM_EPS = 1e-6

LANES = 128
SUBLANES = 8
VMEM_LIMIT = 56 * 1024 * 1024

ROW_TILE = 512
IN_ROW_TILE = 1024
N_CHUNK = 512
LRU_TIME_CHUNK = 256
LRU_DENSE_ROWS = 256
HEAD_PAIRS = ATTN_WIDTH // LANES
QKV_SLABS = 3 * HEAD_PAIRS
LRU_GROUPS = LRU_WIDTH // LANES
ATTN_GROUP = 32
QK_LOOKAHEAD = 4
GATE_PRESCALE = 0.5
SQRT_FLOOR = 1e-37
Q_SCALE = HEAD_DIM ** -0.5 * math.log2(math.e)

bf16 = jnp.bfloat16
f32 = jnp.float32


def _rms(x, g):
    y = x * lax.rsqrt(jnp.mean(x * x, axis=-1, keepdims=True) + NORM_EPS)
    return y * g


def _layer_vec(width, layer):
    return pl.BlockSpec((None, 1, width), lambda i: (layer, 0, 0))


def _in_proj_kernel(x_ref, g_ref, w_ref, wo_ref, w1_ref, w2_ref,
                    qkv_ref, xr_ref, gr_ref, wo_out_ref, w1_out_ref, w2_out_ref):
    h = _rms(x_ref[...], g_ref[...]).astype(bf16)
    slabs_per_chunk = N_CHUNK // LANES
    qkv_chunks = QKV_SLABS // slabs_per_chunk
    for n in range(IN_WIDTH // N_CHUNK):
        z = jnp.dot(h, w_ref[:, n * N_CHUNK:(n + 1) * N_CHUNK], preferred_element_type=f32)
        if n < qkv_chunks:
            for s in range(slabs_per_chunk):
                qkv_ref[0, n * slabs_per_chunk + s] = z[:, s * LANES:(s + 1) * LANES]
        else:
            out_ref = (xr_ref, gr_ref)[(n - qkv_chunks) * N_CHUNK // LRU_WIDTH]
            col = (n - qkv_chunks) * N_CHUNK % LRU_WIDTH
            out_ref[:, col:col + N_CHUNK] = z
    wo_out_ref[...] = wo_ref[...].astype(bf16)
    w1_out_ref[...] = w1_ref[...].astype(bf16)
    w2_out_ref[...] = w2_ref[...].astype(bf16)


def _in_proj(x2d, gains, w_in_bf16, w_out, w_ff_in, w_ff_out, layer, batch, seq):
    t = x2d.shape[0]
    steps = t // IN_ROW_TILE
    tiles_per_seq = seq // IN_ROW_TILE
    rows_in = lambda w: pl.BlockSpec((None, w.shape[1] // steps, w.shape[2]),
                                     lambda i: (layer, i, 0))
    rows_out = lambda w: pl.BlockSpec((w.shape[1] // steps, w.shape[2]), lambda i: (i, 0))
    cast = (w_out, w_ff_in, w_ff_out)
    for w in cast:
        assert w.shape[1] % (steps * 2 * SUBLANES) == 0
    return pl.pallas_call(
        _in_proj_kernel,
        grid=(steps,),
        in_specs=[
            pl.BlockSpec((IN_ROW_TILE, D_MODEL), lambda i: (i, 0)),
            _layer_vec(D_MODEL, layer),
            pl.BlockSpec((None, D_MODEL, IN_WIDTH), lambda i: (layer, 0, 0),
                         pipeline_mode=pl.Buffered(1)),
        ] + [rows_in(w) for w in cast],
        out_specs=[
            pl.BlockSpec((1, QKV_SLABS, IN_ROW_TILE, LANES),
                         lambda i: (i // tiles_per_seq, 0, i % tiles_per_seq, 0)),
            pl.BlockSpec((IN_ROW_TILE, LRU_WIDTH), lambda i: (i, 0)),
            pl.BlockSpec((IN_ROW_TILE, LRU_WIDTH), lambda i: (i, 0)),
        ] + [rows_out(w) for w in cast],
        out_shape=[
            jax.ShapeDtypeStruct((batch, QKV_SLABS, seq, LANES), f32),
            jax.ShapeDtypeStruct((t, LRU_WIDTH), f32),
            jax.ShapeDtypeStruct((t, LRU_WIDTH), f32),
        ] + [jax.ShapeDtypeStruct(w.shape[1:], bf16) for w in cast],
        compiler_params=pltpu.CompilerParams(
            dimension_semantics=("parallel",), vmem_limit_bytes=VMEM_LIMIT),
        name="in_proj",
    )(x2d, gains, w_in_bf16, *cast)


def _rows(start, size, stride=1):
    if stride == 1:
        if not isinstance(start, int):
            start = pl.multiple_of(start, SPAN)
        return pl.ds(start, size)
    return pl.ds(start, size, stride=stride)


def _attn_kernel(q_ref, k_ref, v_ref, o_ref, num_ref, den_ref, mx_ref, q4_ref, k4_ref, v4_ref,
                 band_ref, first_ref):
    seq = q_ref.shape[2]
    slab_rows = seq // 4
    lane = lax.broadcasted_iota(jnp.int32, (1, LANES), 1)
    head_sel = (lane < HEAD_DIM, lane >= HEAD_DIM)
    nt = (((1,), (1,)), ((), ()))

    def ones_sel(h, rows):
        keep = lax.broadcasted_iota(jnp.int32, (rows, LANES), 1) < HEAD_DIM
        return jnp.where(keep if h == 0 else ~keep, 1.0, 0.0).astype(bf16)

    dist = (SPAN + lax.broadcasted_iota(jnp.int32, (SPAN, 2 * SPAN), 0)
            - lax.broadcasted_iota(jnp.int32, (SPAN, 2 * SPAN), 1))
    band_ref[...] = jnp.where((dist >= 0) & (dist <= SPAN), 0.0, -jnp.inf)
    first_ref[...] = jnp.where(lax.broadcasted_iota(jnp.int32, (SPAN, SPAN), 0)
                               >= lax.broadcasted_iota(jnp.int32, (SPAN, SPAN), 1), 0.0, -jnp.inf)

    def make_units(qall, kall, vall, count, has_first, sinks):
        qall = (qall * Q_SCALE).astype(bf16)
        kall, vall = kall.astype(bf16), vall.astype(bf16)
        units = []
        for u in range(count):
            q = qall[u * SPAN:(u + 1) * SPAN]
            if has_first and u == 0:
                k, v, bias = kall[:SPAN], vall[:SPAN], first_ref
            else:
                i = u if has_first else u + 1
                k, v = kall[(i - 1) * SPAN:(i + 1) * SPAN], vall[(i - 1) * SPAN:(i + 1) * SPAN]
                bias = band_ref
            units.append((q, k, v, bias, sinks[u]))
        return units

    def run(units):
        scores = {}

        def issue_qk(i):
            q, k, _, bias, _ = units[i]
            scores[i] = [lax.dot_general(q, jnp.where(sel, k, jnp.zeros_like(k)), nt,
                                         preferred_element_type=f32) + bias[...]
                         for sel in head_sel]

        def finish(i):
            _, _, v, _, (p, slab, rows) = units[i]
            probs, maxes = [], []
            for s in scores.pop(i):
                mx = jnp.max(s, axis=-1, keepdims=True)
                probs.append(jnp.exp2(s - mx).astype(bf16))
                maxes.append(mx)
            vaug = jnp.concatenate(
                [jnp.concatenate([jnp.where(sel, v, jnp.zeros_like(v)), ones_sel(h, v.shape[0])],
                                 axis=1) for h, sel in enumerate(head_sel)], axis=0)
            acc = jnp.dot(jnp.concatenate(probs, axis=1), vaug, preferred_element_type=f32)
            num_ref[p, slab, rows, :] = acc[:, :LANES]
            den_ref[p, slab, rows, :] = acc[:, LANES:]
            mx_ref[p, slab, rows, :] = jnp.where(head_sel[0], maxes[0], maxes[1])

        for i in range(min(QK_LOOKAHEAD, len(units))):
            issue_qk(i)
        for i in range(len(units)):
            if i + QK_LOOKAHEAD < len(units):
                issue_qk(i + QK_LOOKAHEAD)
            finish(i)

    def d1_group(n0, has_first):
        lo = n0 if has_first else n0 - 1
        nk = ATTN_GROUP if has_first else ATTN_GROUP + 1
        sinks = []
        for u in range(ATTN_GROUP):
            start = (n0 + u) * SPAN
            sinks.append((0, start // slab_rows, _rows(start % slab_rows, SPAN)))
        run(make_units(q_ref[0, 0, _rows(n0 * SPAN, ATTN_GROUP * SPAN), :],
                       k_ref[0, 0, _rows(lo * SPAN, nk * SPAN), :],
                       v_ref[0, 0, _rows(lo * SPAN, nk * SPAN), :],
                       ATTN_GROUP, has_first, sinks))

    d1_group(0, True)

    def d1_step(g, carry):
        d1_group(g * ATTN_GROUP, False)
        return carry

    lax.fori_loop(1, seq // (SPAN * ATTN_GROUP), d1_step, 0)

    blocks4 = slab_rows // SPAN
    classes4 = ATTN_GROUP // blocks4

    def d4_step(g, carry):
        units = []
        for cc in range(classes4):
            c = g * classes4 + cc
            qall = q_ref[0, 0, _rows(c, slab_rows, 4), :]
            kall = k_ref[0, 0, _rows(c, slab_rows, 4), :]
            vall = v_ref[0, 0, _rows(c, slab_rows, 4), :]
            q4_ref[c], k4_ref[c], v4_ref[c] = qall, kall, vall
            sinks = [(1, c, _rows(u * SPAN, SPAN)) for u in range(blocks4)]
            units += make_units(qall, kall, vall, blocks4, True, sinks)
        run(units)
        return carry

    lax.fori_loop(0, 4 // classes4, d4_step, 0)

    blocks16 = slab_rows // (4 * SPAN)
    classes16 = ATTN_GROUP // blocks16

    def d16_step(g, carry):
        units = []
        for cc in range(classes16):
            c16 = g * classes16 + cc
            c4, j = c16 % 4, c16 // 4
            rows = _rows(j, blocks16 * SPAN, 4)
            sinks = [(2, c4, _rows(j + 4 * SPAN * u, SPAN, 4)) for u in range(blocks16)]
            units += make_units(q4_ref[c4, rows, :], k4_ref[c4, rows, :], v4_ref[c4, rows, :],
                                blocks16, True, sinks)
        run(units)
        return carry

    lax.fori_loop(0, MAX_DILATION // classes16, d16_step, 0)

    merge_rows = 256
    per_slab = slab_rows // merge_rows

    def merge(i, carry):
        c4 = i // per_slab
        m0 = (i - c4 * per_slab) * merge_rows
        rows = _rows(m0, merge_rows)
        tok = c4 + 4 * m0
        trows = _rows(tok % slab_rows, merge_rows, 4)
        tslab = tok // slab_rows

        def stat(ref, p):
            return ref[p, tslab, trows, :] if p == 0 else ref[p, c4, rows, :]

        mx = [stat(mx_ref, p) for p in range(3)]
        top = jnp.maximum(jnp.maximum(mx[0], mx[1]), mx[2])
        w = [jnp.exp2(m - top) for m in mx]
        num = sum(w[p] * stat(num_ref, p) for p in range(3))
        den = sum(w[p] * stat(den_ref, p) for p in range(3))
        o_ref[0, 0, _rows(tok, merge_rows, 4), :] = num / den
        return carry

    lax.fori_loop(0, 4 * per_slab, merge, 0)


def _attention(qkv):
    b, _, s, _ = qkv.shape
    assert s % (MAX_DILATION * SPAN) == 0 and (s // SPAN) % ATTN_GROUP == 0
    assert ATTN_GROUP % (s // (4 * SPAN)) == 0 and 4 % (ATTN_GROUP // (s // (4 * SPAN))) == 0
    blk = (1, 1, s, LANES)
    stat = pltpu.VMEM((len(DILATIONS), 4, s // 4, LANES), f32)
    mod4 = pltpu.VMEM((4, s // 4, LANES), f32)
    return pl.pallas_call(
        _attn_kernel,
        grid=(b, HEAD_PAIRS),
        in_specs=[
            pl.BlockSpec(blk, lambda i, j: (i, j, 0, 0)),
            pl.BlockSpec(blk, lambda i, j: (i, HEAD_PAIRS + j, 0, 0)),
            pl.BlockSpec(blk, lambda i, j: (i, 2 * HEAD_PAIRS + j, 0, 0)),
        ],
        out_specs=pl.BlockSpec((1, 1, s, LANES), lambda i, j: (i, j, 0, 0)),
        out_shape=jax.ShapeDtypeStruct((b, HEAD_PAIRS, s, LANES), f32),
        scratch_shapes=[
            stat, stat, stat, mod4, mod4, mod4,
            pltpu.VMEM((SPAN, 2 * SPAN), f32),
            pltpu.VMEM((SPAN, SPAN), f32),
        ],
        compiler_params=pltpu.CompilerParams(
            dimension_semantics=("parallel", "parallel"), vmem_limit_bytes=VMEM_LIMIT),
        name="dilated_attention",
    )(qkv, qkv, qkv)


def _sigmoid_of_twice(half_x):
    return 0.5 * jnp.tanh(half_x) + 0.5


def _rglru_kernel(xr_ref, gr_ref, cw_ref, cb_ref, wg_ref, br_ref, bi_ref, lam_ref,
                  o_ref, xi_ref, a_ref, u_ref, h_ref, *, batch):
    tc = LRU_TIME_CHUNK
    halo = (CONV_WIDTH - 1) * batch
    step = pl.program_id(0)

    @pl.when(step == 0)
    def _():
        h_ref[...] = jnp.zeros_like(h_ref)
        xi_ref[:, 0:halo, :] = jnp.zeros((LRU_GROUPS, halo, LANES), f32)

    def interleave(b, carry):
        x = xr_ref[b]
        for g in range(LRU_GROUPS):
            xi_ref[g, pl.ds(halo + b, tc, stride=batch), :] = x[:, g * LANES:(g + 1) * LANES]
        return carry

    lax.fori_loop(0, batch, interleave, 0)

    neg_lam = -lam_ref[...]
    softplus = jnp.maximum(neg_lam, 0.0) + jnp.log1p(jnp.exp(-jnp.abs(neg_lam)))
    rate = -LRU_C * softplus
    half = LRU_WIDTH // 2
    chunk = LRU_DENSE_ROWS

    def dense(ci, carry):
        r0 = pl.multiple_of(ci * chunk, chunk)
        y = cb_ref[...]
        for j in range(CONV_WIDTH):
            tap = CONV_WIDTH - 1 - j
            rows = pl.ds(pl.multiple_of(r0 + halo - j * batch, batch), chunk)
            xs = jnp.concatenate([xi_ref[g, rows, :] for g in range(LRU_GROUPS)], axis=-1)
            y = y + cw_ref[tap:tap + 1, :] * xs
        yb = y.astype(bf16)
        gates = [jnp.dot(yb[:, hf * half:(hf + 1) * half], wg_ref[hf],
                         preferred_element_type=f32) for hf in range(2)]
        pre_r = jnp.concatenate([gates[0][:, :half], gates[1][:, :half]], axis=-1)
        pre_i = jnp.concatenate([gates[0][:, half:], gates[1][:, half:]], axis=-1)
        r = _sigmoid_of_twice(pre_r + br_ref[...])
        i = _sigmoid_of_twice(pre_i + bi_ref[...])
        log_a = rate * r
        a = jnp.exp(log_a)
        gain_sq = -jnp.tanh(log_a) * (a * a + 1.0)
        u = (gain_sq * lax.rsqrt(jnp.maximum(gain_sq, SQRT_FLOOR))) * (i * y)
        for g in range(LRU_GROUPS):
            cols = slice(g * LANES, (g + 1) * LANES)
            a_ref[g, pl.ds(r0, chunk), :] = a[:, cols]
            u_ref[g, pl.ds(r0, chunk), :] = u[:, cols]
        return carry

    lax.fori_loop(0, tc * batch // chunk, dense, 0)
    xi_ref[:, 0:halo, :] = xi_ref[:, tc * batch:tc * batch + halo, :]

    def scan(t, hs):
        rows = pl.ds(pl.multiple_of(t * batch, batch), batch)
        new = []
        for g in range(LRU_GROUPS):
            h = a_ref[g, rows, :] * hs[g] + u_ref[g, rows, :]
            u_ref[g, rows, :] = h
            new.append(h)
        return tuple(new)

    hs = lax.fori_loop(0, tc, scan, tuple(h_ref[g] for g in range(LRU_GROUPS)), unroll=8)
    for g in range(LRU_GROUPS):
        h_ref[g] = hs[g]

    def emit(b, carry):
        h = jnp.concatenate(
            [u_ref[g, pl.ds(b, tc, stride=batch), :] for g in range(LRU_GROUPS)], axis=-1)
        o_ref[b] = (h * jax.nn.gelu(gr_ref[b])).astype(o_ref.dtype)
        return carry

    lax.fori_loop(0, batch, emit, 0)


def _rglru(xr3, gr3, conv_w, conv_b, w_gates, b_r, b_i, lam, layer):
    b, s, _ = xr3.shape
    assert b == SUBLANES
    tc = LRU_TIME_CHUNK
    blk = (b, tc, LRU_WIDTH)
    vec = _layer_vec(LRU_WIDTH, layer)
    return pl.pallas_call(
        functools.partial(_rglru_kernel, batch=b),
        grid=(s // tc,),
        in_specs=[
            pl.BlockSpec(blk, lambda i: (0, i, 0)),
            pl.BlockSpec(blk, lambda i: (0, i, 0)),
            pl.BlockSpec((None, CONV_WIDTH, LRU_WIDTH), lambda i: (layer, 0, 0)),
            vec,
            pl.BlockSpec((None, 2, LRU_WIDTH // 2, LRU_WIDTH), lambda i: (layer, 0, 0, 0)),
            vec, vec, vec,
        ],
        out_specs=pl.BlockSpec(blk, lambda i: (0, i, 0)),
        out_shape=jax.ShapeDtypeStruct((b, s, LRU_WIDTH), bf16),
        scratch_shapes=[
            pltpu.VMEM((LRU_GROUPS, (tc + CONV_WIDTH - 1) * b, LANES), f32),
            pltpu.VMEM((LRU_GROUPS, tc * b, LANES), f32),
            pltpu.VMEM((LRU_GROUPS, tc * b, LANES), f32),
            pltpu.VMEM((LRU_GROUPS, b, LANES), f32),
        ],
        compiler_params=pltpu.CompilerParams(
            dimension_semantics=("arbitrary",), vmem_limit_bytes=VMEM_LIMIT),
        name="rglru",
    )(xr3, gr3, conv_w, conv_b, w_gates, b_r, b_i, lam)


def _gate_weights(w_r, w_i):
    per_half = N_LRU_BLOCKS // 2
    eye = jnp.eye(per_half, dtype=w_r.dtype)

    def dense(w):
        d = w.shape[0]
        blocks = w.reshape(d, 2, per_half, LRU_BLOCK, LRU_BLOCK)
        full = jnp.einsum('dhaij,ab->dhaibj', blocks, eye)
        return full.reshape(d, 2, per_half * LRU_BLOCK, per_half * LRU_BLOCK)

    return (jnp.concatenate([dense(w_r), dense(w_i)], axis=-1) * GATE_PRESCALE).astype(bf16)


def _out_mlp_kernel(*refs):
    attn_refs = refs[:HEAD_PAIRS]
    rec_ref, x_ref, gpost_ref, gpre2_ref, gpost2_ref, wo_ref, w1_ref, w2_ref, o_ref = refs[HEAD_PAIRS:]
    halves = [slice(i * ROW_TILE // 2, (i + 1) * ROW_TILE // 2) for i in range(2)]
    ff_chunk = 2 * N_CHUNK
    mix = [jnp.concatenate([a[r, :].astype(bf16) for a in attn_refs] + [rec_ref[r, :]], axis=-1)
           for r in halves]
    m = [jnp.dot(mx, wo_ref[...], preferred_element_type=f32) for mx in mix]
    x1, y = [], []
    for i, r in enumerate(halves):
        x1.append(x_ref[r, :] + _rms(m[i], gpost_ref[...]))
        h = _rms(x1[i], gpre2_ref[...]).astype(bf16)
        acc = None
        for n in range(D_FF // ff_chunk):
            cols = slice(n * ff_chunk, (n + 1) * ff_chunk)
            a = jnp.maximum(jnp.dot(h, w1_ref[:, cols], preferred_element_type=f32), 0.0)
            part = jnp.dot((a * a).astype(bf16), w2_ref[cols, :], preferred_element_type=f32)
            acc = part if acc is None else acc + part
        y.append(acc)
    for i, r in enumerate(halves):
        o_ref[r, :] = x1[i] + _rms(y[i], gpost2_ref[...])


def _out_mlp(attn, rec2d, x2d, g_post, g_pre2, g_post2, wo, w1, w2, layer):
    t = x2d.shape[0]
    tiles_per_seq = attn.shape[2] // ROW_TILE
    vec = _layer_vec(D_MODEL, layer)
    const = lambda shape: pl.BlockSpec(shape, lambda i: (0, 0), pipeline_mode=pl.Buffered(1))
    return pl.pallas_call(
        _out_mlp_kernel,
        grid=(t // ROW_TILE,),
        in_specs=[
            pl.BlockSpec((None, None, ROW_TILE, LANES),
                         lambda i, hp=hp: (i // tiles_per_seq, hp, i % tiles_per_seq, 0))
            for hp in range(HEAD_PAIRS)
        ] + [
            pl.BlockSpec((ROW_TILE, LRU_WIDTH), lambda i: (i, 0)),
            pl.BlockSpec((ROW_TILE, D_MODEL), lambda i: (i, 0)),
            vec, vec, vec,
            const((ATTN_WIDTH + LRU_WIDTH, D_MODEL)),
            const((D_MODEL, D_FF)),
            const((D_FF, D_MODEL)),
        ],
        out_specs=pl.BlockSpec((ROW_TILE, D_MODEL), lambda i: (i, 0)),
        out_shape=jax.ShapeDtypeStruct((t, D_MODEL), f32),
        compiler_params=pltpu.CompilerParams(
            dimension_semantics=("parallel",), vmem_limit_bytes=VMEM_LIMIT),
        name="out_mlp",
    )(*([attn] * HEAD_PAIRS), rec2d, x2d, g_post, g_pre2, g_post2, wo, w1, w2)


def kernel(x, mix_norm_pre, mix_norm_post, mlp_norm_pre, mlp_norm_post, w_in, conv_w, conv_b,
           w_rgate, b_rgate, w_igate, b_igate, lru_lambda, w_out, w_ff_in, w_ff_out):
    b, s, d = x.shape
    depth = w_in.shape[0]
    t = b * s
    x2d = x.reshape(t, d)
    rows = lambda v: v.reshape(depth, 1, -1)
    w_in_bf16 = w_in.astype(bf16)
    w_gates = _gate_weights(w_rgate, w_igate)
    b_r, b_i = rows(b_rgate) * GATE_PRESCALE, rows(b_igate) * GATE_PRESCALE
    for l in range(depth):
        qkv, xr, gr, wo, w1, w2 = _in_proj(x2d, rows(mix_norm_pre), w_in_bf16, w_out, w_ff_in,
                                           w_ff_out, l, b, s)
        attn = _attention(qkv)
        rec = _rglru(xr.reshape(b, s, LRU_WIDTH), gr.reshape(b, s, LRU_WIDTH), conv_w,
                     rows(conv_b), w_gates, b_r, b_i, rows(lru_lambda), l)
        x2d = _out_mlp(attn, rec.reshape(t, LRU_WIDTH), x2d,
                       rows(mix_norm_post), rows(mlp_norm_pre), rows(mlp_norm_post), wo, w1, w2, l)
    return x2d.reshape(b, s, d)
```

```python
import functools
import math

import jax
import jax.numpy as jnp
from jax import lax
from jax.experimental import pallas as pl
from jax.experimental.pallas import tpu as pltpu

D_MODEL = 1024
N_HEADS = 8
HEAD_DIM = 64
ATTN_WIDTH = N_HEADS * HEAD_DIM
DILATIONS = (1, 4, 16)
MAX_DILATION = 16
SPAN = 128
LRU_WIDTH = 512
N_LRU_BLOCKS = 8
LRU_BLOCK = LRU_WIDTH // N_LRU_BLOCKS
CONV_WIDTH = 4
LRU_C = 8.0
IN_WIDTH = 3 * ATTN_WIDTH + 2 * LRU_WIDTH
D_FF = 4 * D_MODEL
NORM_EPS = 1e-6

LANES = 128
SUBLANES = 8
VMEM_LIMIT = 56 * 1024 * 1024

ROW_TILE = 512
IN_ROW_TILE = 1024
N_CHUNK = 512
LRU_TIME_CHUNK = 256
LRU_DENSE_ROWS = 512
SLABS = DILATIONS[1]
WALK = MAX_DILATION // SLABS
MERGE_ROWS = 1024
HEAD_PAIRS = ATTN_WIDTH // LANES
QKV_SLABS = 3 * HEAD_PAIRS
LRU_GROUPS = LRU_WIDTH // LANES
ATTN_GROUP = 32
QK_LOOKAHEAD = 4
GATE_PRESCALE = 0.5
SQRT_FLOOR = 1e-37
Q_SCALE = HEAD_DIM ** -0.5 * math.log2(math.e)

bf16 = jnp.bfloat16
f32 = jnp.float32


def _rms(x, g):
    y = x * lax.rsqrt(jnp.mean(x * x, axis=-1, keepdims=True) + NORM_EPS)
    return y * g


def _layer_vec(width, layer):
    return pl.BlockSpec((None, 1, width), lambda i: (layer, 0, 0))


def _in_proj_kernel(x_ref, g_ref, w_ref, wo_ref, w1_ref, w2_ref,
                    qkv_ref, xr_ref, gr_ref, wo_out_ref, w1_out_ref, w2_out_ref):
    h = _rms(x_ref[...], g_ref[...]).astype(bf16)
    slabs_per_chunk = N_CHUNK // LANES
    qkv_chunks = QKV_SLABS // slabs_per_chunk
    for n in range(IN_WIDTH // N_CHUNK):
        z = jnp.dot(h, w_ref[:, n * N_CHUNK:(n + 1) * N_CHUNK], preferred_element_type=f32)
        if n < qkv_chunks:
            for s in range(slabs_per_chunk):
                qkv_ref[0, n * slabs_per_chunk + s] = z[:, s * LANES:(s + 1) * LANES]
        else:
            out_ref = (xr_ref, gr_ref)[(n - qkv_chunks) * N_CHUNK // LRU_WIDTH]
            col = (n - qkv_chunks) * N_CHUNK % LRU_WIDTH
            out_ref[:, col:col + N_CHUNK] = z
    wo_out_ref[...] = wo_ref[...].astype(bf16)
    w1_out_ref[...] = w1_ref[...].astype(bf16)
    w2_out_ref[...] = w2_ref[...].astype(bf16)


def _in_proj(x2d, gains, w_in_bf16, w_out, w_ff_in, w_ff_out, layer, batch, seq):
    t = x2d.shape[0]
    steps = t // IN_ROW_TILE
    tiles_per_seq = seq // IN_ROW_TILE
    rows_in = lambda w: pl.BlockSpec((None, w.shape[1] // steps, w.shape[2]),
                                     lambda i: (layer, i, 0))
    rows_out = lambda w: pl.BlockSpec((w.shape[1] // steps, w.shape[2]), lambda i: (i, 0))
    cast = (w_out, w_ff_in, w_ff_out)
    for w in cast:
        assert w.shape[1] % (steps * 2 * SUBLANES) == 0
    return pl.pallas_call(
        _in_proj_kernel,
        grid=(steps,),
        in_specs=[
            pl.BlockSpec((IN_ROW_TILE, D_MODEL), lambda i: (i, 0)),
            _layer_vec(D_MODEL, layer),
            pl.BlockSpec((None, D_MODEL, IN_WIDTH), lambda i: (layer, 0, 0),
                         pipeline_mode=pl.Buffered(1)),
        ] + [rows_in(w) for w in cast],
        out_specs=[
            pl.BlockSpec((1, QKV_SLABS, IN_ROW_TILE, LANES),
                         lambda i: (i // tiles_per_seq, 0, i % tiles_per_seq, 0)),
            pl.BlockSpec((IN_ROW_TILE, LRU_WIDTH), lambda i: (i, 0)),
            pl.BlockSpec((IN_ROW_TILE, LRU_WIDTH), lambda i: (i, 0)),
        ] + [rows_out(w) for w in cast],
        out_shape=[
            jax.ShapeDtypeStruct((batch, QKV_SLABS, seq, LANES), f32),
            jax.ShapeDtypeStruct((t, LRU_WIDTH), f32),
            jax.ShapeDtypeStruct((t, LRU_WIDTH), f32),
        ] + [jax.ShapeDtypeStruct(w.shape[1:], bf16) for w in cast],
        compiler_params=pltpu.CompilerParams(
            dimension_semantics=("parallel",), vmem_limit_bytes=VMEM_LIMIT),
        name="in_proj",
    )(x2d, gains, w_in_bf16, *cast)


def _rows(start, size, stride=1):
    if stride == 1:
        if not isinstance(start, int):
            start = pl.multiple_of(start, SPAN)
        return pl.ds(start, size)
    return pl.ds(start, size, stride=stride)


def _attn_kernel(q_ref, k_ref, v_ref, o_ref, num_ref, den_ref, mx_ref, q4_ref, k4_ref, v4_ref,
                 band_ref, first_ref):
    seq = q_ref.shape[2]
    slab_rows = seq // SLABS
    lane = lax.broadcasted_iota(jnp.int32, (1, LANES), 1)
    head_sel = (lane < HEAD_DIM, lane >= HEAD_DIM)
    nt = (((1,), (1,)), ((), ()))

    def ones_sel(h, rows):
        keep = lax.broadcasted_iota(jnp.int32, (rows, LANES), 1) < HEAD_DIM
        return jnp.where(keep if h == 0 else ~keep, 1.0, 0.0).astype(bf16)

    dist = (SPAN + lax.broadcasted_iota(jnp.int32, (SPAN, 2 * SPAN), 0)
            - lax.broadcasted_iota(jnp.int32, (SPAN, 2 * SPAN), 1))
    band_ref[...] = jnp.where((dist >= 0) & (dist <= SPAN), 0.0, -jnp.inf)
    first_ref[...] = jnp.where(lax.broadcasted_iota(jnp.int32, (SPAN, SPAN), 0)
                               >= lax.broadcasted_iota(jnp.int32, (SPAN, SPAN), 1), 0.0, -jnp.inf)

    def make_units(qall, kall, vall, count, has_first, sinks):
        qall = (qall * Q_SCALE).astype(bf16)
        kall, vall = kall.astype(bf16), vall.astype(bf16)
        units = []
        for u in range(count):
            q = qall[u * SPAN:(u + 1) * SPAN]
            if has_first and u == 0:
                k, v, bias = kall[:SPAN], vall[:SPAN], first_ref
            else:
                i = u if has_first else u + 1
                k, v = kall[(i - 1) * SPAN:(i + 1) * SPAN], vall[(i - 1) * SPAN:(i + 1) * SPAN]
                bias = band_ref
            units.append((q, k, v, bias, sinks[u]))
        return units

    def run(units):
        scores = {}

        def issue_qk(i):
            q, k, _, bias, _ = units[i]
            scores[i] = [lax.dot_general(q, jnp.where(sel, k, jnp.zeros_like(k)), nt,
                                         preferred_element_type=f32) + bias[...]
                         for sel in head_sel]

        def finish(i):
            _, _, v, _, (p, rows) = units[i]
            probs, maxes = [], []
            for s in scores.pop(i):
                mx = jnp.max(s, axis=-1, keepdims=True)
                probs.append(jnp.exp2(s - mx).astype(bf16))
                maxes.append(mx)
            vaug = jnp.concatenate(
                [jnp.concatenate([jnp.where(sel, v, jnp.zeros_like(v)), ones_sel(h, v.shape[0])],
                                 axis=1) for h, sel in enumerate(head_sel)], axis=0)
            acc = jnp.dot(jnp.concatenate(probs, axis=1), vaug, preferred_element_type=f32)
            num_ref[p, rows, :] = acc[:, :LANES]
            den_ref[p, rows, :] = acc[:, LANES:]
            mx_ref[p, rows, :] = jnp.where(head_sel[0], maxes[0], maxes[1])

        for i in range(min(QK_LOOKAHEAD, len(units))):
            issue_qk(i)
        for i in range(len(units)):
            if i + QK_LOOKAHEAD < len(units):
                issue_qk(i + QK_LOOKAHEAD)
            finish(i)

    def d1_group(n0, has_first):
        lo = n0 if has_first else n0 - 1
        nk = ATTN_GROUP if has_first else ATTN_GROUP + 1
        sinks = [(0, _rows((n0 + u) * SPAN, SPAN)) for u in range(ATTN_GROUP)]
        run(make_units(q_ref[0, 0, _rows(n0 * SPAN, ATTN_GROUP * SPAN), :],
                       k_ref[0, 0, _rows(lo * SPAN, nk * SPAN), :],
                       v_ref[0, 0, _rows(lo * SPAN, nk * SPAN), :],
                       ATTN_GROUP, has_first, sinks))

    d1_group(0, True)

    def d1_step(g, carry):
        d1_group(g * ATTN_GROUP, False)
        return carry

    lax.fori_loop(1, seq // (SPAN * ATTN_GROUP), d1_step, 0)

    blocks4 = slab_rows // SPAN
    classes4 = ATTN_GROUP // blocks4

    def d4_step(g, carry):
        units = []
        for cc in range(classes4):
            c = g * classes4 + cc
            qall = q_ref[0, 0, _rows(c, slab_rows, SLABS), :]
            kall = k_ref[0, 0, _rows(c, slab_rows, SLABS), :]
            vall = v_ref[0, 0, _rows(c, slab_rows, SLABS), :]
            q4_ref[c], k4_ref[c], v4_ref[c] = qall, kall, vall
            sinks = [(1, _rows(c * slab_rows + u * SPAN, SPAN)) for u in range(blocks4)]
            units += make_units(qall, kall, vall, blocks4, True, sinks)
        run(units)
        return carry

    lax.fori_loop(0, SLABS // classes4, d4_step, 0)

    blocks16 = slab_rows // (WALK * SPAN)
    classes16 = ATTN_GROUP // blocks16

    def d16_step(g, carry):
        units = []
        for cc in range(classes16):
            c16 = g * classes16 + cc
            c4, j = c16 % SLABS, c16 // SLABS
            rows = _rows(j, blocks16 * SPAN, WALK)
            sinks = [(2, _rows(c4 * slab_rows + j + WALK * SPAN * u, SPAN, WALK))
                     for u in range(blocks16)]
            units += make_units(q4_ref[c4, rows, :], k4_ref[c4, rows, :], v4_ref[c4, rows, :],
                                blocks16, True, sinks)
        run(units)
        return carry

    lax.fori_loop(0, MAX_DILATION // classes16, d16_step, 0)

    merge_rows = min(MERGE_ROWS, slab_rows)
    per_slab = slab_rows // merge_rows

    def merge(i, carry):
        c4 = i // per_slab
        m0 = (i - c4 * per_slab) * merge_rows
        rows = _rows(c4 * slab_rows + m0, merge_rows)
        tok = c4 + SLABS * m0
        trows = _rows(tok, merge_rows, SLABS)

        def stat(ref, p):
            return ref[p, trows if p == 0 else rows, :]

        mx = [stat(mx_ref, p) for p in range(3)]
        top = jnp.maximum(jnp.maximum(mx[0], mx[1]), mx[2])
        w = [jnp.exp2(m - top) for m in mx]
        num = sum(w[p] * stat(num_ref, p) for p in range(3))
        den = sum(w[p] * stat(den_ref, p) for p in range(3))
        o_ref[0, 0, trows, :] = num / den
        return carry

    lax.fori_loop(0, SLABS * per_slab, merge, 0)


def _attention(qkv):
    b, _, s, _ = qkv.shape
    assert s % (MAX_DILATION * SPAN) == 0 and (s // SPAN) % ATTN_GROUP == 0
    blocks4 = s // (SLABS * SPAN)
    assert ATTN_GROUP % blocks4 == 0 and SLABS % (ATTN_GROUP // blocks4) == 0
    blk = (1, 1, s, LANES)
    stat = pltpu.VMEM((len(DILATIONS), s, LANES), f32)
    mod4 = pltpu.VMEM((SLABS, s // SLABS, LANES), f32)
    return pl.pallas_call(
        _attn_kernel,
        grid=(b, HEAD_PAIRS),
        in_specs=[
            pl.BlockSpec(blk, lambda i, j: (i, j, 0, 0)),
            pl.BlockSpec(blk, lambda i, j: (i, HEAD_PAIRS + j, 0, 0)),
            pl.BlockSpec(blk, lambda i, j: (i, 2 * HEAD_PAIRS + j, 0, 0)),
        ],
        out_specs=pl.BlockSpec((1, 1, s, LANES), lambda i, j: (i, j, 0, 0)),
        out_shape=jax.ShapeDtypeStruct((b, HEAD_PAIRS, s, LANES), f32),
        scratch_shapes=[
            stat, stat, stat, mod4, mod4, mod4,
            pltpu.VMEM((SPAN, 2 * SPAN), f32),
            pltpu.VMEM((SPAN, SPAN), f32),
        ],
        compiler_params=pltpu.CompilerParams(
            dimension_semantics=("parallel", "parallel"), vmem_limit_bytes=VMEM_LIMIT),
        name="dilated_attention",
    )(qkv, qkv, qkv)


def _sigmoid_of_twice(half_x):
    return 0.5 * jnp.tanh(half_x) + 0.5


def _rglru_kernel(xr_ref, gr_ref, cw_ref, cb_ref, wg_ref, br_ref, bi_ref, lam_ref,
                  o_ref, xi_ref, a_ref, u_ref, h_ref, *, batch):
    tc = LRU_TIME_CHUNK
    halo = (CONV_WIDTH - 1) * batch
    step = pl.program_id(0)

    @pl.when(step == 0)
    def _():
        h_ref[...] = jnp.zeros_like(h_ref)
        xi_ref[:, 0:halo, :] = jnp.zeros((LRU_GROUPS, halo, LANES), f32)

    def interleave(b, carry):
        x = xr_ref[b]
        for g in range(LRU_GROUPS):
            xi_ref[g, pl.ds(halo + b, tc, stride=batch), :] = x[:, g * LANES:(g + 1) * LANES]
        return carry

    lax.fori_loop(0, batch, interleave, 0)

    neg_lam = -lam_ref[...]
    softplus = jnp.maximum(neg_lam, 0.0) + jnp.log1p(jnp.exp(-jnp.abs(neg_lam)))
    rate = -LRU_C * softplus
    half = LRU_WIDTH // 2
    chunk = LRU_DENSE_ROWS

    def dense(ci, carry):
        r0 = pl.multiple_of(ci * chunk, chunk)
        y = cb_ref[...]
        for j in range(CONV_WIDTH):
            tap = CONV_WIDTH - 1 - j
            rows = pl.ds(pl.multiple_of(r0 + halo - j * batch, batch), chunk)
            xs = jnp.concatenate([xi_ref[g, rows, :] for g in range(LRU_GROUPS)], axis=-1)
            y = y + cw_ref[tap:tap + 1, :] * xs
        yb = y.astype(bf16)
        gates = [jnp.dot(yb[:, hf * half:(hf + 1) * half], wg_ref[hf],
                         preferred_element_type=f32) for hf in range(2)]
        pre_r = jnp.concatenate([gates[0][:, :half], gates[1][:, :half]], axis=-1)
        pre_i = jnp.concatenate([gates[0][:, half:], gates[1][:, half:]], axis=-1)
        r = _sigmoid_of_twice(pre_r + br_ref[...])
        i = _sigmoid_of_twice(pre_i + bi_ref[...])
        log_a = rate * r
        a = jnp.exp(log_a)
        gain_sq = -jnp.tanh(log_a) * (a * a + 1.0)
        u = (gain_sq * lax.rsqrt(jnp.maximum(gain_sq, SQRT_FLOOR))) * (i * y)
        for g in range(LRU_GROUPS):
            cols = slice(g * LANES, (g + 1) * LANES)
            a_ref[g, pl.ds(r0, chunk), :] = a[:, cols]
            u_ref[g, pl.ds(r0, chunk), :] = u[:, cols]
        return carry

    lax.fori_loop(0, tc * batch // chunk, dense, 0)
    xi_ref[:, 0:halo, :] = xi_ref[:, tc * batch:tc * batch + halo, :]

    def scan(t, hs):
        rows = pl.ds(pl.multiple_of(t * batch, batch), batch)
        new = []
        for g in range(LRU_GROUPS):
            h = a_ref[g, rows, :] * hs[g] + u_ref[g, rows, :]
            u_ref[g, rows, :] = h
            new.append(h)
        return tuple(new)

    hs = lax.fori_loop(0, tc, scan, tuple(h_ref[g] for g in range(LRU_GROUPS)), unroll=8)
    for g in range(LRU_GROUPS):
        h_ref[g] = hs[g]

    def emit(b, carry):
        h = jnp.concatenate(
            [u_ref[g, pl.ds(b, tc, stride=batch), :] for g in range(LRU_GROUPS)], axis=-1)
        o_ref[b] = (h * jax.nn.gelu(gr_ref[b])).astype(o_ref.dtype)
        return carry

    lax.fori_loop(0, batch, emit, 0)


def _rglru(xr3, gr3, conv_w, conv_b, w_gates, b_r, b_i, lam, layer):
    b, s, _ = xr3.shape
    assert b == SUBLANES
    tc = LRU_TIME_CHUNK
    blk = (b, tc, LRU_WIDTH)
    vec = _layer_vec(LRU_WIDTH, layer)
    return pl.pallas_call(
        functools.partial(_rglru_kernel, batch=b),
        grid=(s // tc,),
        in_specs=[
            pl.BlockSpec(blk, lambda i: (0, i, 0)),
            pl.BlockSpec(blk, lambda i: (0, i, 0)),
            pl.BlockSpec((None, CONV_WIDTH, LRU_WIDTH), lambda i: (layer, 0, 0)),
            vec,
            pl.BlockSpec((None, 2, LRU_WIDTH // 2, LRU_WIDTH), lambda i: (layer, 0, 0, 0)),
            vec, vec, vec,
        ],
        out_specs=pl.BlockSpec(blk, lambda i: (0, i, 0)),
        out_shape=jax.ShapeDtypeStruct((b, s, LRU_WIDTH), bf16),
        scratch_shapes=[
            pltpu.VMEM((LRU_GROUPS, (tc + CONV_WIDTH - 1) * b, LANES), f32),
            pltpu.VMEM((LRU_GROUPS, tc * b, LANES), f32),
            pltpu.VMEM((LRU_GROUPS, tc * b, LANES), f32),
            pltpu.VMEM((LRU_GROUPS, b, LANES), f32),
        ],
        compiler_params=pltpu.CompilerParams(
            dimension_semantics=("arbitrary",), vmem_limit_bytes=VMEM_LIMIT),
        name="rglru",
    )(xr3, gr3, conv_w, conv_b, w_gates, b_r, b_i, lam)


def _gate_weights(w_r, w_i):
    per_half = N_LRU_BLOCKS // 2
    eye = jnp.eye(per_half, dtype=w_r.dtype)

    def dense(w):
        d = w.shape[0]
        blocks = w.reshape(d, 2, per_half, LRU_BLOCK, LRU_BLOCK)
        full = jnp.einsum('dhaij,ab->dhaibj', blocks, eye)
        return full.reshape(d, 2, per_half * LRU_BLOCK, per_half * LRU_BLOCK)

    return (jnp.concatenate([dense(w_r), dense(w_i)], axis=-1) * GATE_PRESCALE).astype(bf16)


def _out_mlp_kernel(*refs):
    attn_refs = refs[:HEAD_PAIRS]
    rec_ref, x_ref, gpost_ref, gpre2_ref, gpost2_ref, wo_ref, w1_ref, w2_ref, o_ref = refs[HEAD_PAIRS:]
    halves = [slice(i * ROW_TILE // 2, (i + 1) * ROW_TILE // 2) for i in range(2)]
    ff_chunk = 2 * N_CHUNK
    mix = [jnp.concatenate([a[r, :].astype(bf16) for a in attn_refs] + [rec_ref[r, :]], axis=-1)
           for r in halves]
    m = [jnp.dot(mx, wo_ref[...], preferred_element_type=f32) for mx in mix]
    x1, y = [], []
    for i, r in enumerate(halves):
        x1.append(x_ref[r, :] + _rms(m[i], gpost_ref[...]))
        h = _rms(x1[i], gpre2_ref[...]).astype(bf16)
        acc = None
        for n in range(D_FF // ff_chunk):
            cols = slice(n * ff_chunk, (n + 1) * ff_chunk)
            a = jnp.maximum(jnp.dot(h, w1_ref[:, cols], preferred_element_type=f32), 0.0)
            part = jnp.dot((a * a).astype(bf16), w2_ref[cols, :], preferred_element_type=f32)
            acc = part if acc is None else acc + part
        y.append(acc)
    for i, r in enumerate(halves):
        o_ref[r, :] = x1[i] + _rms(y[i], gpost2_ref[...])


def _out_mlp(attn, rec2d, x2d, g_post, g_pre2, g_post2, wo, w1, w2, layer):
    t = x2d.shape[0]
    tiles_per_seq = attn.shape[2] // ROW_TILE
    vec = _layer_vec(D_MODEL, layer)
    const = lambda shape: pl.BlockSpec(shape, lambda i: (0, 0), pipeline_mode=pl.Buffered(1))
    return pl.pallas_call(
        _out_mlp_kernel,
        grid=(t // ROW_TILE,),
        in_specs=[
            pl.BlockSpec((None, None, ROW_TILE, LANES),
                         lambda i, hp=hp: (i // tiles_per_seq, hp, i % tiles_per_seq, 0))
            for hp in range(HEAD_PAIRS)
        ] + [
            pl.BlockSpec((ROW_TILE, LRU_WIDTH), lambda i: (i, 0)),
            pl.BlockSpec((ROW_TILE, D_MODEL), lambda i: (i, 0)),
            vec, vec, vec,
            const((ATTN_WIDTH + LRU_WIDTH, D_MODEL)),
            const((D_MODEL, D_FF)),
            const((D_FF, D_MODEL)),
        ],
        out_specs=pl.BlockSpec((ROW_TILE, D_MODEL), lambda i: (i, 0)),
        out_shape=jax.ShapeDtypeStruct((t, D_MODEL), f32),
        compiler_params=pltpu.CompilerParams(
            dimension_semantics=("parallel",), vmem_limit_bytes=VMEM_LIMIT),
        name="out_mlp",
    )(*([attn] * HEAD_PAIRS), rec2d, x2d, g_post, g_pre2, g_post2, wo, w1, w2)


def kernel(x, mix_norm_pre, mix_norm_post, mlp_norm_pre, mlp_norm_post, w_in, conv_w, conv_b,
           w_rgate, b_rgate, w_igate, b_igate, lru_lambda, w_out, w_ff_in, w_ff_out):
    b, s, d = x.shape
    depth = w_in.shape[0]
    t = b * s
    x2d = x.reshape(t, d)
    rows = lambda v: v.reshape(depth, 1, -1)
    w_in_bf16 = w_in.astype(bf16)
    w_gates = _gate_weights(w_rgate, w_igate)
    b_r, b_i = rows(b_rgate) * GATE_PRESCALE, rows(b_igate) * GATE_PRESCALE
    for l in range(depth):
        qkv, xr, gr, wo, w1, w2 = _in_proj(x2d, rows(mix_norm_pre), w_in_bf16, w_out, w_ff_in,
                                           w_ff_out, l, b, s)
        attn = _attention(qkv)
        rec = _rglru(xr.reshape(b, s, LRU_WIDTH), gr.reshape(b, s, LRU_WIDTH), conv_w,
                     rows(conv_b), w_gates, b_r, b_i, rows(lru_lambda), l)
        x2d = _out_mlp(attn, rec.reshape(t, LRU_WIDTH), x2d,
                       rows(mix_norm_post), rows(mlp_norm_pre), rows(mlp_norm_post), wo, w1, w2, l)
    return x2d.reshape(b, s, d)
```

```python
import functools
import math

import jax
import jax.numpy as jnp
from jax import lax
from jax.experimental import pallas as pl
from jax.experimental.pallas import tpu as pltpu

D_MODEL = 1024
N_HEADS = 8
HEAD_DIM = 64
ATTN_WIDTH = N_HEADS * HEAD_DIM
DILATIONS = (1, 4, 16)
MAX_DILATION = 16
SPAN = 128
LRU_WIDTH = 512
N_LRU_BLOCKS = 8
LRU_BLOCK = LRU_WIDTH // N_LRU_BLOCKS
CONV_WIDTH = 4
LRU_C = 8.0
IN_WIDTH = 3 * ATTN_WIDTH + 2 * LRU_WIDTH
D_FF = 4 * D_MODEL
NORM_EPS = 1e-6

LANES = 128
SUBLANES = 8
VMEM_LIMIT = 56 * 1024 * 1024

ROW_TILE = 512
IN_ROW_TILE = 1024
N_CHUNK = 512
LRU_TIME_CHUNK = 256
LRU_DENSE_ROWS = 1024
LRU_SCAN_UNROLL = 32
SLABS = DILATIONS[1]
WALK = MAX_DILATION // SLABS
MERGE_ROWS = 1024
HEAD_PAIRS = ATTN_WIDTH // LANES
QKV_SLABS = 3 * HEAD_PAIRS
LRU_GROUPS = LRU_WIDTH // LANES
ATTN_GROUP = 32
QK_LOOKAHEAD = 4
GATE_PRESCALE = 0.5
SQRT_FLOOR = 1e-37
Q_SCALE = HEAD_DIM ** -0.5 * math.log2(math.e)

bf16 = jnp.bfloat16
f32 = jnp.float32


def _rms(x, g):
    y = x * lax.rsqrt(jnp.mean(x * x, axis=-1, keepdims=True) + NORM_EPS)
    return y * g


def _layer_vec(width, layer):
    return pl.BlockSpec((None, 1, width), lambda i: (layer, 0, 0))


def _in_proj_kernel(x_ref, g_ref, w_ref, wo_ref, w1_ref, w2_ref,
                    qkv_ref, xr_ref, gr_ref, wo_out_ref, w1_out_ref, w2_out_ref):
    h = _rms(x_ref[...], g_ref[...]).astype(bf16)
    slabs_per_chunk = N_CHUNK // LANES
    qkv_chunks = QKV_SLABS // slabs_per_chunk
    for n in range(IN_WIDTH // N_CHUNK):
        z = jnp.dot(h, w_ref[:, n * N_CHUNK:(n + 1) * N_CHUNK], preferred_element_type=f32)
        if n < qkv_chunks:
            for s in range(slabs_per_chunk):
                qkv_ref[0, n * slabs_per_chunk + s] = z[:, s * LANES:(s + 1) * LANES]
        else:
            out_ref = (xr_ref, gr_ref)[(n - qkv_chunks) * N_CHUNK // LRU_WIDTH]
            col = (n - qkv_chunks) * N_CHUNK % LRU_WIDTH
            out_ref[:, col:col + N_CHUNK] = z
    wo_out_ref[...] = wo_ref[...].astype(bf16)
    w1_out_ref[...] = w1_ref[...].astype(bf16)
    w2_out_ref[...] = w2_ref[...].astype(bf16)


def _in_proj(x2d, gains, w_in_bf16, w_out, w_ff_in, w_ff_out, layer, batch, seq):
    t = x2d.shape[0]
    steps = t // IN_ROW_TILE
    tiles_per_seq = seq // IN_ROW_TILE
    rows_in = lambda w: pl.BlockSpec((None, w.shape[1] // steps, w.shape[2]),
                                     lambda i: (layer, i, 0))
    rows_out = lambda w: pl.BlockSpec((w.shape[1] // steps, w.shape[2]), lambda i: (i, 0))
    cast = (w_out, w_ff_in, w_ff_out)
    for w in cast:
        assert w.shape[1] % (steps * 2 * SUBLANES) == 0
    return pl.pallas_call(
        _in_proj_kernel,
        grid=(steps,),
        in_specs=[
            pl.BlockSpec((IN_ROW_TILE, D_MODEL), lambda i: (i, 0)),
            _layer_vec(D_MODEL, layer),
            pl.BlockSpec((None, D_MODEL, IN_WIDTH), lambda i: (layer, 0, 0),
                         pipeline_mode=pl.Buffered(1)),
        ] + [rows_in(w) for w in cast],
        out_specs=[
            pl.BlockSpec((1, QKV_SLABS, IN_ROW_TILE, LANES),
                         lambda i: (i // tiles_per_seq, 0, i % tiles_per_seq, 0)),
            pl.BlockSpec((IN_ROW_TILE, LRU_WIDTH), lambda i: (i, 0)),
            pl.BlockSpec((IN_ROW_TILE, LRU_WIDTH), lambda i: (i, 0)),
        ] + [rows_out(w) for w in cast],
        out_shape=[
            jax.ShapeDtypeStruct((batch, QKV_SLABS, seq, LANES), f32),
            jax.ShapeDtypeStruct((t, LRU_WIDTH), f32),
            jax.ShapeDtypeStruct((t, LRU_WIDTH), f32),
        ] + [jax.ShapeDtypeStruct(w.shape[1:], bf16) for w in cast],
        compiler_params=pltpu.CompilerParams(
            dimension_semantics=("parallel",), vmem_limit_bytes=VMEM_LIMIT),
        name="in_proj",
    )(x2d, gains, w_in_bf16, *cast)


def _rows(start, size, stride=1):
    if stride == 1:
        if not isinstance(start, int):
            start = pl.multiple_of(start, SPAN)
        return pl.ds(start, size)
    return pl.ds(start, size, stride=stride)


def _attn_kernel(q_ref, k_ref, v_ref, o_ref, num_ref, den_ref, mx_ref, q4_ref, k4_ref, v4_ref,
                 band_ref, first_ref):
    seq = q_ref.shape[2]
    slab_rows = seq // SLABS
    lane = lax.broadcasted_iota(jnp.int32, (1, LANES), 1)
    head_sel = (lane < HEAD_DIM, lane >= HEAD_DIM)
    nt = (((1,), (1,)), ((), ()))

    def ones_sel(h, rows):
        keep = lax.broadcasted_iota(jnp.int32, (rows, LANES), 1) < HEAD_DIM
        return jnp.where(keep if h == 0 else ~keep, 1.0, 0.0).astype(bf16)

    dist = (SPAN + lax.broadcasted_iota(jnp.int32, (SPAN, 2 * SPAN), 0)
            - lax.broadcasted_iota(jnp.int32, (SPAN, 2 * SPAN), 1))
    band_ref[...] = jnp.where((dist >= 0) & (dist <= SPAN), 0.0, -jnp.inf)
    first_ref[...] = jnp.where(lax.broadcasted_iota(jnp.int32, (SPAN, SPAN), 0)
                               >= lax.broadcasted_iota(jnp.int32, (SPAN, SPAN), 1), 0.0, -jnp.inf)

    def make_units(qall, kall, vall, count, has_first, sinks):
        qall = (qall * Q_SCALE).astype(bf16)
        kall, vall = kall.astype(bf16), vall.astype(bf16)
        units = []
        for u in range(count):
            q = qall[u * SPAN:(u + 1) * SPAN]
            if has_first and u == 0:
                k, v, bias = kall[:SPAN], vall[:SPAN], first_ref
            else:
                i = u if has_first else u + 1
                k, v = kall[(i - 1) * SPAN:(i + 1) * SPAN], vall[(i - 1) * SPAN:(i + 1) * SPAN]
                bias = band_ref
            units.append((q, k, v, bias, sinks[u]))
        return units

    def run(units):
        scores = {}

        def issue_qk(i):
            q, k, _, bias, _ = units[i]
            scores[i] = [lax.dot_general(q, jnp.where(sel, k, jnp.zeros_like(k)), nt,
                                         preferred_element_type=f32) + bias[...]
                         for sel in head_sel]

        def finish(i):
            _, _, v, _, (p, rows) = units[i]
            probs, maxes = [], []
            for s in scores.pop(i):
                mx = jnp.max(s, axis=-1, keepdims=True)
                probs.append(jnp.exp2(s - mx).astype(bf16))
                maxes.append(mx)
            vaug = jnp.concatenate(
                [jnp.concatenate([jnp.where(sel, v, jnp.zeros_like(v)), ones_sel(h, v.shape[0])],
                                 axis=1) for h, sel in enumerate(head_sel)], axis=0)
            acc = jnp.dot(jnp.concatenate(probs, axis=1), vaug, preferred_element_type=f32)
            num_ref[p, rows, :] = acc[:, :LANES]
            den_ref[p, rows, :] = acc[:, LANES:]
            mx_ref[p, rows, :] = jnp.where(head_sel[0], maxes[0], maxes[1])

        for i in range(min(QK_LOOKAHEAD, len(units))):
            issue_qk(i)
        for i in range(len(units)):
            if i + QK_LOOKAHEAD < len(units):
                issue_qk(i + QK_LOOKAHEAD)
            finish(i)

    def d1_group(n0, has_first):
        lo = n0 if has_first else n0 - 1
        nk = ATTN_GROUP if has_first else ATTN_GROUP + 1
        sinks = [(0, _rows((n0 + u) * SPAN, SPAN)) for u in range(ATTN_GROUP)]
        run(make_units(q_ref[0, 0, _rows(n0 * SPAN, ATTN_GROUP * SPAN), :],
                       k_ref[0, 0, _rows(lo * SPAN, nk * SPAN), :],
                       v_ref[0, 0, _rows(lo * SPAN, nk * SPAN), :],
                       ATTN_GROUP, has_first, sinks))

    d1_group(0, True)

    def d1_step(g, carry):
        d1_group(g * ATTN_GROUP, False)
        return carry

    lax.fori_loop(1, seq // (SPAN * ATTN_GROUP), d1_step, 0)

    blocks4 = slab_rows // SPAN
    classes4 = ATTN_GROUP // blocks4

    def d4_step(g, carry):
        units = []
        for cc in range(classes4):
            c = g * classes4 + cc
            qall = q_ref[0, 0, _rows(c, slab_rows, SLABS), :]
            kall = k_ref[0, 0, _rows(c, slab_rows, SLABS), :]
            vall = v_ref[0, 0, _rows(c, slab_rows, SLABS), :]
            q4_ref[c], k4_ref[c], v4_ref[c] = qall, kall, vall
            sinks = [(1, _rows(c * slab_rows + u * SPAN, SPAN)) for u in range(blocks4)]
            units += make_units(qall, kall, vall, blocks4, True, sinks)
        run(units)
        return carry

    lax.fori_loop(0, SLABS // classes4, d4_step, 0)

    blocks16 = slab_rows // (WALK * SPAN)
    classes16 = ATTN_GROUP // blocks16

    def d16_step(g, carry):
        units = []
        for cc in range(classes16):
            c16 = g * classes16 + cc
            c4, j = c16 % SLABS, c16 // SLABS
            rows = _rows(j, blocks16 * SPAN, WALK)
            sinks = [(2, _rows(c4 * slab_rows + j + WALK * SPAN * u, SPAN, WALK))
                     for u in range(blocks16)]
            units += make_units(q4_ref[c4, rows, :], k4_ref[c4, rows, :], v4_ref[c4, rows, :],
                                blocks16, True, sinks)
        run(units)
        return carry

    lax.fori_loop(0, MAX_DILATION // classes16, d16_step, 0)

    merge_rows = min(MERGE_ROWS, slab_rows)
    per_slab = slab_rows // merge_rows

    def merge(i, carry):
        c4 = i // per_slab
        m0 = (i - c4 * per_slab) * merge_rows
        rows = _rows(c4 * slab_rows + m0, merge_rows)
        tok = c4 + SLABS * m0
        trows = _rows(tok, merge_rows, SLABS)

        def stat(ref, p):
            return ref[p, trows if p == 0 else rows, :]

        mx = [stat(mx_ref, p) for p in range(3)]
        top = jnp.maximum(jnp.maximum(mx[0], mx[1]), mx[2])
        w = [jnp.exp2(m - top) for m in mx]
        num = sum(w[p] * stat(num_ref, p) for p in range(3))
        den = sum(w[p] * stat(den_ref, p) for p in range(3))
        o_ref[0, 0, trows, :] = num / den
        return carry

    lax.fori_loop(0, SLABS * per_slab, merge, 0)


def _attention(qkv):
    b, _, s, _ = qkv.shape
    assert s % (MAX_DILATION * SPAN) == 0 and (s // SPAN) % ATTN_GROUP == 0
    blocks4 = s // (SLABS * SPAN)
    assert ATTN_GROUP % blocks4 == 0 and SLABS % (ATTN_GROUP // blocks4) == 0
    blk = (1, 1, s, LANES)
    stat = pltpu.VMEM((len(DILATIONS), s, LANES), f32)
    mod4 = pltpu.VMEM((SLABS, s // SLABS, LANES), f32)
    return pl.pallas_call(
        _attn_kernel,
        grid=(b, HEAD_PAIRS),
        in_specs=[
            pl.BlockSpec(blk, lambda i, j: (i, j, 0, 0)),
            pl.BlockSpec(blk, lambda i, j: (i, HEAD_PAIRS + j, 0, 0)),
            pl.BlockSpec(blk, lambda i, j: (i, 2 * HEAD_PAIRS + j, 0, 0)),
        ],
        out_specs=pl.BlockSpec((1, 1, s, LANES), lambda i, j: (i, j, 0, 0)),
        out_shape=jax.ShapeDtypeStruct((b, HEAD_PAIRS, s, LANES), f32),
        scratch_shapes=[
            stat, stat, stat, mod4, mod4, mod4,
            pltpu.VMEM((SPAN, 2 * SPAN), f32),
            pltpu.VMEM((SPAN, SPAN), f32),
        ],
        compiler_params=pltpu.CompilerParams(
            dimension_semantics=("parallel", "parallel"), vmem_limit_bytes=VMEM_LIMIT),
        name="dilated_attention",
    )(qkv, qkv, qkv)


def _sigmoid_of_twice(half_x):
    return 0.5 * jnp.tanh(half_x) + 0.5


def _rglru_kernel(xr_ref, gr_ref, cw_ref, cb_ref, wg_ref, br_ref, bi_ref, lam_ref,
                  o_ref, xi_ref, a_ref, u_ref, h_ref, *, batch):
    tc = LRU_TIME_CHUNK
    halo = (CONV_WIDTH - 1) * batch
    step = pl.program_id(0)

    @pl.when(step == 0)
    def _():
        h_ref[...] = jnp.zeros_like(h_ref)
        xi_ref[:, 0:halo, :] = jnp.zeros((LRU_GROUPS, halo, LANES), f32)

    def interleave(b, carry):
        x = xr_ref[b]
        for g in range(LRU_GROUPS):
            xi_ref[g, pl.ds(halo + b, tc, stride=batch), :] = x[:, g * LANES:(g + 1) * LANES]
        return carry

    lax.fori_loop(0, batch, interleave, 0, unroll=True)

    neg_lam = -lam_ref[...]
    softplus = jnp.maximum(neg_lam, 0.0) + jnp.log1p(jnp.exp(-jnp.abs(neg_lam)))
    rate = -LRU_C * softplus
    half = LRU_WIDTH // 2
    chunk = LRU_DENSE_ROWS

    def dense(ci, carry):
        r0 = pl.multiple_of(ci * chunk, chunk)
        y = cb_ref[...]
        for j in range(CONV_WIDTH):
            tap = CONV_WIDTH - 1 - j
            rows = pl.ds(pl.multiple_of(r0 + halo - j * batch, batch), chunk)
            xs = jnp.concatenate([xi_ref[g, rows, :] for g in range(LRU_GROUPS)], axis=-1)
            y = y + cw_ref[tap:tap + 1, :] * xs
        yb = y.astype(bf16)
        gates = [jnp.dot(yb[:, hf * half:(hf + 1) * half], wg_ref[hf],
                         preferred_element_type=f32) for hf in range(2)]
        pre_r = jnp.concatenate([gates[0][:, :half], gates[1][:, :half]], axis=-1)
        pre_i = jnp.concatenate([gates[0][:, half:], gates[1][:, half:]], axis=-1)
        r = _sigmoid_of_twice(pre_r + br_ref[...])
        i = _sigmoid_of_twice(pre_i + bi_ref[...])
        log_a = rate * r
        a = jnp.exp(log_a)
        gain_sq = -jnp.tanh(log_a) * (a * a + 1.0)
        u = (gain_sq * lax.rsqrt(jnp.maximum(gain_sq, SQRT_FLOOR))) * (i * y)
        for g in range(LRU_GROUPS):
            cols = slice(g * LANES, (g + 1) * LANES)
            a_ref[g, pl.ds(r0, chunk), :] = a[:, cols]
            u_ref[g, pl.ds(r0, chunk), :] = u[:, cols]
        return carry

    lax.fori_loop(0, tc * batch // chunk, dense, 0)
    xi_ref[:, 0:halo, :] = xi_ref[:, tc * batch:tc * batch + halo, :]

    def scan(t, hs):
        rows = pl.ds(pl.multiple_of(t * batch, batch), batch)
        new = []
        for g in range(LRU_GROUPS):
            h = a_ref[g, rows, :] * hs[g] + u_ref[g, rows, :]
            u_ref[g, rows, :] = h
            new.append(h)
        return tuple(new)

    hs = lax.fori_loop(0, tc, scan, tuple(h_ref[g] for g in range(LRU_GROUPS)),
                       unroll=LRU_SCAN_UNROLL)
    for g in range(LRU_GROUPS):
        h_ref[g] = hs[g]

    def emit(b, carry):
        h = jnp.concatenate(
            [u_ref[g, pl.ds(b, tc, stride=batch), :] for g in range(LRU_GROUPS)], axis=-1)
        o_ref[b] = (h * jax.nn.gelu(gr_ref[b])).astype(o_ref.dtype)
        return carry

    lax.fori_loop(0, batch, emit, 0, unroll=True)


def _rglru(xr3, gr3, conv_w, conv_b, w_gates, b_r, b_i, lam, layer):
    b, s, _ = xr3.shape
    assert b == SUBLANES
    tc = LRU_TIME_CHUNK
    blk = (b, tc, LRU_WIDTH)
    vec = _layer_vec(LRU_WIDTH, layer)
    return pl.pallas_call(
        functools.partial(_rglru_kernel, batch=b),
        grid=(s // tc,),
        in_specs=[
            pl.BlockSpec(blk, lambda i: (0, i, 0)),
            pl.BlockSpec(blk, lambda i: (0, i, 0)),
            pl.BlockSpec((None, CONV_WIDTH, LRU_WIDTH), lambda i: (layer, 0, 0)),
            vec,
            pl.BlockSpec((None, 2, LRU_WIDTH // 2, LRU_WIDTH), lambda i: (layer, 0, 0, 0)),
            vec, vec, vec,
        ],
        out_specs=pl.BlockSpec(blk, lambda i: (0, i, 0)),
        out_shape=jax.ShapeDtypeStruct((b, s, LRU_WIDTH), bf16),
        scratch_shapes=[
            pltpu.VMEM((LRU_GROUPS, (tc + CONV_WIDTH - 1) * b, LANES), f32),
            pltpu.VMEM((LRU_GROUPS, tc * b, LANES), f32),
            pltpu.VMEM((LRU_GROUPS, tc * b, LANES), f32),
            pltpu.VMEM((LRU_GROUPS, b, LANES), f32),
        ],
        compiler_params=pltpu.CompilerParams(
            dimension_semantics=("arbitrary",), vmem_limit_bytes=VMEM_LIMIT),
        name="rglru",
    )(xr3, gr3, conv_w, conv_b, w_gates, b_r, b_i, lam)


def _gate_weights(w_r, w_i):
    per_half = N_LRU_BLOCKS // 2
    eye = jnp.eye(per_half, dtype=w_r.dtype)

    def dense(w):
        d = w.shape[0]
        blocks = w.reshape(d, 2, per_half, LRU_BLOCK, LRU_BLOCK)
        full = jnp.einsum('dhaij,ab->dhaibj', blocks, eye)
        return full.reshape(d, 2, per_half * LRU_BLOCK, per_half * LRU_BLOCK)

    return (jnp.concatenate([dense(w_r), dense(w_i)], axis=-1) * GATE_PRESCALE).astype(bf16)


def _out_mlp_kernel(*refs):
    attn_refs = refs[:HEAD_PAIRS]
    rec_ref, x_ref, gpost_ref, gpre2_ref, gpost2_ref, wo_ref, w1_ref, w2_ref, o_ref = refs[HEAD_PAIRS:]
    halves = [slice(i * ROW_TILE // 2, (i + 1) * ROW_TILE // 2) for i in range(2)]
    ff_chunk = 2 * N_CHUNK
    mix = [jnp.concatenate([a[r, :].astype(bf16) for a in attn_refs] + [rec_ref[r, :]], axis=-1)
           for r in halves]
    m = [jnp.dot(mx, wo_ref[...], preferred_element_type=f32) for mx in mix]
    x1, y = [], []
    for i, r in enumerate(halves):
        x1.append(x_ref[r, :] + _rms(m[i], gpost_ref[...]))
        h = _rms(x1[i], gpre2_ref[...]).astype(bf16)
        acc = None
        for n in range(D_FF // ff_chunk):
            cols = slice(n * ff_chunk, (n + 1) * ff_chunk)
            a = jnp.maximum(jnp.dot(h, w1_ref[:, cols], preferred_element_type=f32), 0.0)
            part = jnp.dot((a * a).astype(bf16), w2_ref[cols, :], preferred_element_type=f32)
            acc = part if acc is None else acc + part
        y.append(acc)
    for i, r in enumerate(halves):
        o_ref[r, :] = x1[i] + _rms(y[i], gpost2_ref[...])


def _out_mlp(attn, rec2d, x2d, g_post, g_pre2, g_post2, wo, w1, w2, layer):
    t = x2d.shape[0]
    tiles_per_seq = attn.shape[2] // ROW_TILE
    vec = _layer_vec(D_MODEL, layer)
    const = lambda shape: pl.BlockSpec(shape, lambda i: (0, 0), pipeline_mode=pl.Buffered(1))
    return pl.pallas_call(
        _out_mlp_kernel,
        grid=(t // ROW_TILE,),
        in_specs=[
            pl.BlockSpec((None, None, ROW_TILE, LANES),
                         lambda i, hp=hp: (i // tiles_per_seq, hp, i % tiles_per_seq, 0))
            for hp in range(HEAD_PAIRS)
        ] + [
            pl.BlockSpec((ROW_TILE, LRU_WIDTH), lambda i: (i, 0)),
            pl.BlockSpec((ROW_TILE, D_MODEL), lambda i: (i, 0)),
            vec, vec, vec,
            const((ATTN_WIDTH + LRU_WIDTH, D_MODEL)),
            const((D_MODEL, D_FF)),
            const((D_FF, D_MODEL)),
        ],
        out_specs=pl.BlockSpec((ROW_TILE, D_MODEL), lambda i: (i, 0)),
        out_shape=jax.ShapeDtypeStruct((t, D_MODEL), f32),
        compiler_params=pltpu.CompilerParams(
            dimension_semantics=("parallel",), vmem_limit_bytes=VMEM_LIMIT),
        name="out_mlp",
    )(*([attn] * HEAD_PAIRS), rec2d, x2d, g_post, g_pre2, g_post2, wo, w1, w2)


def kernel(x, mix_norm_pre, mix_norm_post, mlp_norm_pre, mlp_norm_post, w_in, conv_w, conv_b,
           w_rgate, b_rgate, w_igate, b_igate, lru_lambda, w_out, w_ff_in, w_ff_out):
    b, s, d = x.shape
    depth = w_in.shape[0]
    t = b * s
    x2d = x.reshape(t, d)
    rows = lambda v: v.reshape(depth, 1, -1)
    w_in_bf16 = w_in.astype(bf16)
    w_gates = _gate_weights(w_rgate, w_igate)
    b_r, b_i = rows(b_rgate) * GATE_PRESCALE, rows(b_igate) * GATE_PRESCALE
    for l in range(depth):
        qkv, xr, gr, wo, w1, w2 = _in_proj(x2d, rows(mix_norm_pre), w_in_bf16, w_out, w_ff_in,
                                           w_ff_out, l, b, s)
        attn = _attention(qkv)
        rec = _rglru(xr.reshape(b, s, LRU_WIDTH), gr.reshape(b, s, LRU_WIDTH), conv_w,
                     rows(conv_b), w_gates, b_r, b_i, rows(lru_lambda), l)
        x2d = _out_mlp(attn, rec.reshape(t, LRU_WIDTH), x2d,
                       rows(mix_norm_post), rows(mlp_norm_pre), rows(mlp_norm_post), wo, w1, w2, l)
    return x2d.reshape(b, s, d)
```

```python
import functools
import math

import jax
import jax.numpy as jnp
from jax import lax
from jax.experimental import pallas as pl
from jax.experimental.pallas import tpu as pltpu

D_MODEL = 1024
N_HEADS = 8
HEAD_DIM = 64
ATTN_WIDTH = N_HEADS * HEAD_DIM
DILATIONS = (1, 4, 16)
MAX_DILATION = 16
SPAN = 128
LRU_WIDTH = 512
N_LRU_BLOCKS = 8
LRU_BLOCK = LRU_WIDTH // N_LRU_BLOCKS
CONV_WIDTH = 4
LRU_C = 8.0
IN_WIDTH = 3 * ATTN_WIDTH + 2 * LRU_WIDTH
D_FF = 4 * D_MODEL
NORM_EPS = 1e-6

LANES = 128
SUBLANES = 8
VMEM_LIMIT = 56 * 1024 * 1024

ROW_TILE = 512
IN_ROW_TILE = 1024
N_CHUNK = 512
LRU_TIME_CHUNK = 256
LRU_DENSE_ROWS = 2048
LRU_SCAN_UNROLL = 256
SLABS = DILATIONS[1]
WALK = MAX_DILATION // SLABS
MERGE_ROWS = 1024
HEAD_PAIRS = ATTN_WIDTH // LANES
QKV_SLABS = 3 * HEAD_PAIRS
LRU_GROUPS = LRU_WIDTH // LANES
ATTN_GROUP = 32
QK_LOOKAHEAD = 4
GATE_PRESCALE = 0.5
SQRT_FLOOR = 1e-37
Q_SCALE = HEAD_DIM ** -0.5 * math.log2(math.e)

bf16 = jnp.bfloat16
f32 = jnp.float32


def _rms(x, g):
    y = x * lax.rsqrt(jnp.mean(x * x, axis=-1, keepdims=True) + NORM_EPS)
    return y * g


def _layer_vec(width, layer):
    return pl.BlockSpec((None, 1, width), lambda i: (layer, 0, 0))


def _in_proj_kernel(x_ref, g_ref, w_ref, wo_ref, w1_ref, w2_ref,
                    qkv_ref, xr_ref, gr_ref, wo_out_ref, w1_out_ref, w2_out_ref):
    h = _rms(x_ref[...], g_ref[...]).astype(bf16)
    slabs_per_chunk = N_CHUNK // LANES
    qkv_chunks = QKV_SLABS // slabs_per_chunk
    for n in range(IN_WIDTH // N_CHUNK):
        z = jnp.dot(h, w_ref[:, n * N_CHUNK:(n + 1) * N_CHUNK], preferred_element_type=f32)
        if n < qkv_chunks:
            for s in range(slabs_per_chunk):
                qkv_ref[0, n * slabs_per_chunk + s] = z[:, s * LANES:(s + 1) * LANES]
        else:
            out_ref = (xr_ref, gr_ref)[(n - qkv_chunks) * N_CHUNK // LRU_WIDTH]
            col = (n - qkv_chunks) * N_CHUNK % LRU_WIDTH
            out_ref[:, col:col + N_CHUNK] = z
    wo_out_ref[...] = wo_ref[...].astype(bf16)
    w1_out_ref[...] = w1_ref[...].astype(bf16)
    w2_out_ref[...] = w2_ref[...].astype(bf16)


def _in_proj(x2d, gains, w_in_bf16, w_out, w_ff_in, w_ff_out, layer, batch, seq):
    t = x2d.shape[0]
    steps = t // IN_ROW_TILE
    tiles_per_seq = seq // IN_ROW_TILE
    rows_in = lambda w: pl.BlockSpec((None, w.shape[1] // steps, w.shape[2]),
                                     lambda i: (layer, i, 0))
    rows_out = lambda w: pl.BlockSpec((w.shape[1] // steps, w.shape[2]), lambda i: (i, 0))
    cast = (w_out, w_ff_in, w_ff_out)
    for w in cast:
        assert w.shape[1] % (steps * 2 * SUBLANES) == 0
    return pl.pallas_call(
        _in_proj_kernel,
        grid=(steps,),
        in_specs=[
            pl.BlockSpec((IN_ROW_TILE, D_MODEL), lambda i: (i, 0)),
            _layer_vec(D_MODEL, layer),
            pl.BlockSpec((None, D_MODEL, IN_WIDTH), lambda i: (layer, 0, 0),
                         pipeline_mode=pl.Buffered(1)),
        ] + [rows_in(w) for w in cast],
        out_specs=[
            pl.BlockSpec((1, QKV_SLABS, IN_ROW_TILE, LANES),
                         lambda i: (i // tiles_per_seq, 0, i % tiles_per_seq, 0)),
            pl.BlockSpec((IN_ROW_TILE, LRU_WIDTH), lambda i: (i, 0)),
            pl.BlockSpec((IN_ROW_TILE, LRU_WIDTH), lambda i: (i, 0)),
        ] + [rows_out(w) for w in cast],
        out_shape=[
            jax.ShapeDtypeStruct((batch, QKV_SLABS, seq, LANES), f32),
            jax.ShapeDtypeStruct((t, LRU_WIDTH), f32),
            jax.ShapeDtypeStruct((t, LRU_WIDTH), f32),
        ] + [jax.ShapeDtypeStruct(w.shape[1:], bf16) for w in cast],
        compiler_params=pltpu.CompilerParams(
            dimension_semantics=("parallel",), vmem_limit_bytes=VMEM_LIMIT),
        name="in_proj",
    )(x2d, gains, w_in_bf16, *cast)


def _rows(start, size, stride=1):
    if stride == 1:
        if not isinstance(start, int):
            start = pl.multiple_of(start, SPAN)
        return pl.ds(start, size)
    return pl.ds(start, size, stride=stride)


def _attn_kernel(q_ref, k_ref, v_ref, o_ref, num_ref, den_ref, mx_ref, q4_ref, k4_ref, v4_ref,
                 band_ref, first_ref):
    seq = q_ref.shape[2]
    slab_rows = seq // SLABS
    lane = lax.broadcasted_iota(jnp.int32, (1, LANES), 1)
    head_sel = (lane < HEAD_DIM, lane >= HEAD_DIM)
    nt = (((1,), (1,)), ((), ()))

    def ones_sel(h, rows):
        keep = lax.broadcasted_iota(jnp.int32, (rows, LANES), 1) < HEAD_DIM
        return jnp.where(keep if h == 0 else ~keep, 1.0, 0.0).astype(bf16)

    dist = (SPAN + lax.broadcasted_iota(jnp.int32, (SPAN, 2 * SPAN), 0)
            - lax.broadcasted_iota(jnp.int32, (SPAN, 2 * SPAN), 1))
    band_ref[...] = jnp.where((dist >= 0) & (dist <= SPAN), 0.0, -jnp.inf)
    first_ref[...] = jnp.where(lax.broadcasted_iota(jnp.int32, (SPAN, SPAN), 0)
                               >= lax.broadcasted_iota(jnp.int32, (SPAN, SPAN), 1), 0.0, -jnp.inf)

    def make_units(qall, kall, vall, count, has_first, sinks):
        qall = (qall * Q_SCALE).astype(bf16)
        kall, vall = kall.astype(bf16), vall.astype(bf16)
        units = []
        for u in range(count):
            q = qall[u * SPAN:(u + 1) * SPAN]
            if has_first and u == 0:
                k, v, bias = kall[:SPAN], vall[:SPAN], first_ref
            else:
                i = u if has_first else u + 1
                k, v = kall[(i - 1) * SPAN:(i + 1) * SPAN], vall[(i - 1) * SPAN:(i + 1) * SPAN]
                bias = band_ref
            units.append((q, k, v, bias, sinks[u]))
        return units

    def run(units):
        scores = {}

        def issue_qk(i):
            q, k, _, bias, _ = units[i]
            scores[i] = [lax.dot_general(q, jnp.where(sel, k, jnp.zeros_like(k)), nt,
                                         preferred_element_type=f32) + bias[...]
                         for sel in head_sel]

        def finish(i):
            _, _, v, _, (p, rows) = units[i]
            probs, maxes = [], []
            for s in scores.pop(i):
                mx = jnp.max(s, axis=-1, keepdims=True)
                probs.append(jnp.exp2(s - mx).astype(bf16))
                maxes.append(mx)
            vaug = jnp.concatenate(
                [jnp.concatenate([jnp.where(sel, v, jnp.zeros_like(v)), ones_sel(h, v.shape[0])],
                                 axis=1) for h, sel in enumerate(head_sel)], axis=0)
            acc = jnp.dot(jnp.concatenate(probs, axis=1), vaug, preferred_element_type=f32)
            num_ref[p, rows, :] = acc[:, :LANES]
            den_ref[p, rows, :] = acc[:, LANES:]
            mx_ref[p, rows, :] = jnp.where(head_sel[0], maxes[0], maxes[1])

        for i in range(min(QK_LOOKAHEAD, len(units))):
            issue_qk(i)
        for i in range(len(units)):
            if i + QK_LOOKAHEAD < len(units):
                issue_qk(i + QK_LOOKAHEAD)
            finish(i)

    def d1_group(n0, has_first):
        lo = n0 if has_first else n0 - 1
        nk = ATTN_GROUP if has_first else ATTN_GROUP + 1
        sinks = [(0, _rows((n0 + u) * SPAN, SPAN)) for u in range(ATTN_GROUP)]
        run(make_units(q_ref[0, 0, _rows(n0 * SPAN, ATTN_GROUP * SPAN), :],
                       k_ref[0, 0, _rows(lo * SPAN, nk * SPAN), :],
                       v_ref[0, 0, _rows(lo * SPAN, nk * SPAN), :],
                       ATTN_GROUP, has_first, sinks))

    d1_group(0, True)

    def d1_step(g, carry):
        d1_group(g * ATTN_GROUP, False)
        return carry

    lax.fori_loop(1, seq // (SPAN * ATTN_GROUP), d1_step, 0)

    blocks4 = slab_rows // SPAN
    classes4 = ATTN_GROUP // blocks4

    def d4_step(g, carry):
        units = []
        for cc in range(classes4):
            c = g * classes4 + cc
            qall = q_ref[0, 0, _rows(c, slab_rows, SLABS), :]
            kall = k_ref[0, 0, _rows(c, slab_rows, SLABS), :]
            vall = v_ref[0, 0, _rows(c, slab_rows, SLABS), :]
            q4_ref[c], k4_ref[c], v4_ref[c] = qall, kall, vall
            sinks = [(1, _rows(c * slab_rows + u * SPAN, SPAN)) for u in range(blocks4)]
            units += make_units(qall, kall, vall, blocks4, True, sinks)
        run(units)
        return carry

    lax.fori_loop(0, SLABS // classes4, d4_step, 0)

    blocks16 = slab_rows // (WALK * SPAN)
    classes16 = ATTN_GROUP // blocks16

    def d16_step(g, carry):
        units = []
        for cc in range(classes16):
            c16 = g * classes16 + cc
            c4, j = c16 % SLABS, c16 // SLABS
            rows = _rows(j, blocks16 * SPAN, WALK)
            sinks = [(2, _rows(c4 * slab_rows + j + WALK * SPAN * u, SPAN, WALK))
                     for u in range(blocks16)]
            units += make_units(q4_ref[c4, rows, :], k4_ref[c4, rows, :], v4_ref[c4, rows, :],
                                blocks16, True, sinks)
        run(units)
        return carry

    lax.fori_loop(0, MAX_DILATION // classes16, d16_step, 0)

    merge_rows = min(MERGE_ROWS, slab_rows)
    per_slab = slab_rows // merge_rows

    def merge(i, carry):
        c4 = i // per_slab
        m0 = (i - c4 * per_slab) * merge_rows
        rows = _rows(c4 * slab_rows + m0, merge_rows)
        tok = c4 + SLABS * m0
        trows = _rows(tok, merge_rows, SLABS)

        def stat(ref, p):
            return ref[p, trows if p == 0 else rows, :]

        mx = [stat(mx_ref, p) for p in range(3)]
        top = jnp.maximum(jnp.maximum(mx[0], mx[1]), mx[2])
        w = [jnp.exp2(m - top) for m in mx]
        num = sum(w[p] * stat(num_ref, p) for p in range(3))
        den = sum(w[p] * stat(den_ref, p) for p in range(3))
        o_ref[0, 0, trows, :] = num / den
        return carry

    lax.fori_loop(0, SLABS * per_slab, merge, 0)


def _attention(qkv):
    b, _, s, _ = qkv.shape
    assert s % (MAX_DILATION * SPAN) == 0 and (s // SPAN) % ATTN_GROUP == 0
    blocks4 = s // (SLABS * SPAN)
    assert ATTN_GROUP % blocks4 == 0 and SLABS % (ATTN_GROUP // blocks4) == 0
    blk = (1, 1, s, LANES)
    stat = pltpu.VMEM((len(DILATIONS), s, LANES), f32)
    mod4 = pltpu.VMEM((SLABS, s // SLABS, LANES), f32)
    return pl.pallas_call(
        _attn_kernel,
        grid=(b, HEAD_PAIRS),
        in_specs=[
            pl.BlockSpec(blk, lambda i, j: (i, j, 0, 0)),
            pl.BlockSpec(blk, lambda i, j: (i, HEAD_PAIRS + j, 0, 0)),
            pl.BlockSpec(blk, lambda i, j: (i, 2 * HEAD_PAIRS + j, 0, 0)),
        ],
        out_specs=pl.BlockSpec((1, 1, s, LANES), lambda i, j: (i, j, 0, 0)),
        out_shape=jax.ShapeDtypeStruct((b, HEAD_PAIRS, s, LANES), f32),
        scratch_shapes=[
            stat, stat, stat, mod4, mod4, mod4,
            pltpu.VMEM((SPAN, 2 * SPAN), f32),
            pltpu.VMEM((SPAN, SPAN), f32),
        ],
        compiler_params=pltpu.CompilerParams(
            dimension_semantics=("parallel", "parallel"), vmem_limit_bytes=VMEM_LIMIT),
        name="dilated_attention",
    )(qkv, qkv, qkv)


def _sigmoid_of_twice(half_x):
    return 0.5 * jnp.tanh(half_x) + 0.5


def _rglru_kernel(xr_ref, gr_ref, cw_ref, cb_ref, wg_ref, br_ref, bi_ref, lam_ref,
                  o_ref, xi_ref, a_ref, u_ref, h_ref, *, batch):
    tc = LRU_TIME_CHUNK
    halo = (CONV_WIDTH - 1) * batch
    step = pl.program_id(0)

    @pl.when(step == 0)
    def _():
        h_ref[...] = jnp.zeros_like(h_ref)
        xi_ref[:, 0:halo, :] = jnp.zeros((LRU_GROUPS, halo, LANES), f32)

    def interleave(b, carry):
        x = xr_ref[b]
        for g in range(LRU_GROUPS):
            xi_ref[g, pl.ds(halo + b, tc, stride=batch), :] = x[:, g * LANES:(g + 1) * LANES]
        return carry

    lax.fori_loop(0, batch, interleave, 0, unroll=True)

    neg_lam = -lam_ref[...]
    softplus = jnp.maximum(neg_lam, 0.0) + jnp.log1p(jnp.exp(-jnp.abs(neg_lam)))
    rate = -LRU_C * softplus
    half = LRU_WIDTH // 2
    chunk = LRU_DENSE_ROWS

    def dense(ci, carry):
        r0 = pl.multiple_of(ci * chunk, chunk)
        y = cb_ref[...]
        for j in range(CONV_WIDTH):
            tap = CONV_WIDTH - 1 - j
            rows = pl.ds(pl.multiple_of(r0 + halo - j * batch, batch), chunk)
            xs = jnp.concatenate([xi_ref[g, rows, :] for g in range(LRU_GROUPS)], axis=-1)
            y = y + cw_ref[tap:tap + 1, :] * xs
        yb = y.astype(bf16)
        gates = [jnp.dot(yb[:, hf * half:(hf + 1) * half], wg_ref[hf],
                         preferred_element_type=f32) for hf in range(2)]
        pre_r = jnp.concatenate([gates[0][:, :half], gates[1][:, :half]], axis=-1)
        pre_i = jnp.concatenate([gates[0][:, half:], gates[1][:, half:]], axis=-1)
        r = _sigmoid_of_twice(pre_r + br_ref[...])
        i = _sigmoid_of_twice(pre_i + bi_ref[...])
        log_a = rate * r
        a = jnp.exp(log_a)
        gain_sq = -jnp.tanh(log_a) * (a * a + 1.0)
        u = (gain_sq * lax.rsqrt(jnp.maximum(gain_sq, SQRT_FLOOR))) * (i * y)
        for g in range(LRU_GROUPS):
            cols = slice(g * LANES, (g + 1) * LANES)
            a_ref[g, pl.ds(r0, chunk), :] = a[:, cols]
            u_ref[g, pl.ds(r0, chunk), :] = u[:, cols]
        return carry

    lax.fori_loop(0, tc * batch // chunk, dense, 0)
    xi_ref[:, 0:halo, :] = xi_ref[:, tc * batch:tc * batch + halo, :]

    def scan(t, hs):
        rows = pl.ds(pl.multiple_of(t * batch, batch), batch)
        new = []
        for g in range(LRU_GROUPS):
            h = a_ref[g, rows, :] * hs[g] + u_ref[g, rows, :]
            u_ref[g, rows, :] = h
            new.append(h)
        return tuple(new)

    hs = lax.fori_loop(0, tc, scan, tuple(h_ref[g] for g in range(LRU_GROUPS)),
                       unroll=LRU_SCAN_UNROLL)
    for g in range(LRU_GROUPS):
        h_ref[g] = hs[g]

    def emit(b, carry):
        h = jnp.concatenate(
            [u_ref[g, pl.ds(b, tc, stride=batch), :] for g in range(LRU_GROUPS)], axis=-1)
        o_ref[b] = (h * jax.nn.gelu(gr_ref[b])).astype(o_ref.dtype)
        return carry

    lax.fori_loop(0, batch, emit, 0, unroll=True)


def _rglru(xr3, gr3, conv_w, conv_b, w_gates, b_r, b_i, lam, layer):
    b, s, _ = xr3.shape
    assert b == SUBLANES
    tc = LRU_TIME_CHUNK
    blk = (b, tc, LRU_WIDTH)
    vec = _layer_vec(LRU_WIDTH, layer)
    return pl.pallas_call(
        functools.partial(_rglru_kernel, batch=b),
        grid=(s // tc,),
        in_specs=[
            pl.BlockSpec(blk, lambda i: (0, i, 0)),
            pl.BlockSpec(blk, lambda i: (0, i, 0)),
            pl.BlockSpec((None, CONV_WIDTH, LRU_WIDTH), lambda i: (layer, 0, 0)),
            vec,
            pl.BlockSpec((None, 2, LRU_WIDTH // 2, LRU_WIDTH), lambda i: (layer, 0, 0, 0)),
            vec, vec, vec,
        ],
        out_specs=pl.BlockSpec(blk, lambda i: (0, i, 0)),
        out_shape=jax.ShapeDtypeStruct((b, s, LRU_WIDTH), bf16),
        scratch_shapes=[
            pltpu.VMEM((LRU_GROUPS, (tc + CONV_WIDTH - 1) * b, LANES), f32),
            pltpu.VMEM((LRU_GROUPS, tc * b, LANES), f32),
            pltpu.VMEM((LRU_GROUPS, tc * b, LANES), f32),
            pltpu.VMEM((LRU_GROUPS, b, LANES), f32),
        ],
        compiler_params=pltpu.CompilerParams(
            dimension_semantics=("arbitrary",), vmem_limit_bytes=VMEM_LIMIT),
        name="rglru",
    )(xr3, gr3, conv_w, conv_b, w_gates, b_r, b_i, lam)


def _gate_weights(w_r, w_i):
    per_half = N_LRU_BLOCKS // 2
    eye = jnp.eye(per_half, dtype=w_r.dtype)

    def dense(w):
        d = w.shape[0]
        blocks = w.reshape(d, 2, per_half, LRU_BLOCK, LRU_BLOCK)
        full = jnp.einsum('dhaij,ab->dhaibj', blocks, eye)
        return full.reshape(d, 2, per_half * LRU_BLOCK, per_half * LRU_BLOCK)

    return (jnp.concatenate([dense(w_r), dense(w_i)], axis=-1) * GATE_PRESCALE).astype(bf16)


def _out_mlp_kernel(*refs):
    attn_refs = refs[:HEAD_PAIRS]
    rec_ref, x_ref, gpost_ref, gpre2_ref, gpost2_ref, wo_ref, w1_ref, w2_ref, o_ref = refs[HEAD_PAIRS:]
    halves = [slice(i * ROW_TILE // 2, (i + 1) * ROW_TILE // 2) for i in range(2)]
    ff_chunk = 2 * N_CHUNK
    mix = [jnp.concatenate([a[r, :].astype(bf16) for a in attn_refs] + [rec_ref[r, :]], axis=-1)
           for r in halves]
    m = [jnp.dot(mx, wo_ref[...], preferred_element_type=f32) for mx in mix]
    x1, y = [], []
    for i, r in enumerate(halves):
        x1.append(x_ref[r, :] + _rms(m[i], gpost_ref[...]))
        h = _rms(x1[i], gpre2_ref[...]).astype(bf16)
        acc = None
        for n in range(D_FF // ff_chunk):
            cols = slice(n * ff_chunk, (n + 1) * ff_chunk)
            a = jnp.maximum(jnp.dot(h, w1_ref[:, cols], preferred_element_type=f32), 0.0)
            part = jnp.dot((a * a).astype(bf16), w2_ref[cols, :], preferred_element_type=f32)
            acc = part if acc is None else acc + part
        y.append(acc)
    for i, r in enumerate(halves):
        o_ref[r, :] = x1[i] + _rms(y[i], gpost2_ref[...])


def _out_mlp(attn, rec2d, x2d, g_post, g_pre2, g_post2, wo, w1, w2, layer):
    t = x2d.shape[0]
    tiles_per_seq = attn.shape[2] // ROW_TILE
    vec = _layer_vec(D_MODEL, layer)
    const = lambda shape: pl.BlockSpec(shape, lambda i: (0, 0), pipeline_mode=pl.Buffered(1))
    return pl.pallas_call(
        _out_mlp_kernel,
        grid=(t // ROW_TILE,),
        in_specs=[
            pl.BlockSpec((None, None, ROW_TILE, LANES),
                         lambda i, hp=hp: (i // tiles_per_seq, hp, i % tiles_per_seq, 0))
            for hp in range(HEAD_PAIRS)
        ] + [
            pl.BlockSpec((ROW_TILE, LRU_WIDTH), lambda i: (i, 0)),
            pl.BlockSpec((ROW_TILE, D_MODEL), lambda i: (i, 0)),
            vec, vec, vec,
            const((ATTN_WIDTH + LRU_WIDTH, D_MODEL)),
            const((D_MODEL, D_FF)),
            const((D_FF, D_MODEL)),
        ],
        out_specs=pl.BlockSpec((ROW_TILE, D_MODEL), lambda i: (i, 0)),
        out_shape=jax.ShapeDtypeStruct((t, D_MODEL), f32),
        compiler_params=pltpu.CompilerParams(
            dimension_semantics=("parallel",), vmem_limit_bytes=VMEM_LIMIT),
        name="out_mlp",
    )(*([attn] * HEAD_PAIRS), rec2d, x2d, g_post, g_pre2, g_post2, wo, w1, w2)


def kernel(x, mix_norm_pre, mix_norm_post, mlp_norm_pre, mlp_norm_post, w_in, conv_w, conv_b,
           w_rgate, b_rgate, w_igate, b_igate, lru_lambda, w_out, w_ff_in, w_ff_out):
    b, s, d = x.shape
    depth = w_in.shape[0]
    t = b * s
    x2d = x.reshape(t, d)
    rows = lambda v: v.reshape(depth, 1, -1)
    w_in_bf16 = w_in.astype(bf16)
    w_gates = _gate_weights(w_rgate, w_igate)
    b_r, b_i = rows(b_rgate) * GATE_PRESCALE, rows(b_igate) * GATE_PRESCALE
    for l in range(depth):
        qkv, xr, gr, wo, w1, w2 = _in_proj(x2d, rows(mix_norm_pre), w_in_bf16, w_out, w_ff_in,
                                           w_ff_out, l, b, s)
        attn = _attention(qkv)
        rec = _rglru(xr.reshape(b, s, LRU_WIDTH), gr.reshape(b, s, LRU_WIDTH), conv_w,
                     rows(conv_b), w_gates, b_r, b_i, rows(lru_lambda), l)
        x2d = _out_mlp(attn, rec.reshape(t, LRU_WIDTH), x2d,
                       rows(mix_norm_post), rows(mlp_norm_pre), rows(mlp_norm_post), wo, w1, w2, l)
    return x2d.reshape(b, s, d)
```

```python
import functools
import math

import jax
import jax.numpy as jnp
from jax import lax
from jax.experimental import pallas as pl
from jax.experimental.pallas import tpu as pltpu

D_MODEL = 1024
N_HEADS = 8
HEAD_DIM = 64
ATTN_WIDTH = N_HEADS * HEAD_DIM
DILATIONS = (1, 4, 16)
MAX_DILATION = 16
SPAN = 128
LRU_WIDTH = 512
N_LRU_BLOCKS = 8
LRU_BLOCK = LRU_WIDTH // N_LRU_BLOCKS
CONV_WIDTH = 4
LRU_C = 8.0
IN_WIDTH = 3 * ATTN_WIDTH + 2 * LRU_WIDTH
D_FF = 4 * D_MODEL
NORM_EPS = 1e-6

LANES = 128
SUBLANES = 8
VMEM_LIMIT = 56 * 1024 * 1024

ROW_TILE = 1024
MLP_SUB_ROWS = 256
IN_ROW_TILE = 1024
N_CHUNK = 512
LRU_TIME_CHUNK = 256
LRU_DENSE_ROWS = 2048
LRU_SCAN_UNROLL = 256
SLABS = DILATIONS[1]
WALK = MAX_DILATION // SLABS
MERGE_ROWS = 1024
HEAD_PAIRS = ATTN_WIDTH // LANES
QKV_SLABS = 3 * HEAD_PAIRS
LRU_GROUPS = LRU_WIDTH // LANES
ATTN_GROUP = 32
QK_LOOKAHEAD = 4
GATE_PRESCALE = 0.5
SQRT_FLOOR = 1e-37
Q_SCALE = HEAD_DIM ** -0.5 * math.log2(math.e)

bf16 = jnp.bfloat16
f32 = jnp.float32


def _rms(x, g):
    y = x * lax.rsqrt(jnp.mean(x * x, axis=-1, keepdims=True) + NORM_EPS)
    return y * g


def _layer_vec(width, layer):
    return pl.BlockSpec((None, 1, width), lambda i: (layer, 0, 0))


def _in_proj_kernel(x_ref, g_ref, w_ref, wo_ref, w1_ref, w2_ref,
                    qkv_ref, xr_ref, gr_ref, wo_out_ref, w1_out_ref, w2_out_ref):
    h = _rms(x_ref[...], g_ref[...]).astype(bf16)
    slabs_per_chunk = N_CHUNK // LANES
    qkv_chunks = QKV_SLABS // slabs_per_chunk
    for n in range(IN_WIDTH // N_CHUNK):
        z = jnp.dot(h, w_ref[:, n * N_CHUNK:(n + 1) * N_CHUNK], preferred_element_type=f32)
        if n < qkv_chunks:
            for s in range(slabs_per_chunk):
                qkv_ref[0, n * slabs_per_chunk + s] = z[:, s * LANES:(s + 1) * LANES]
        else:
            out_ref = (xr_ref, gr_ref)[(n - qkv_chunks) * N_CHUNK // LRU_WIDTH]
            col = (n - qkv_chunks) * N_CHUNK % LRU_WIDTH
            out_ref[:, col:col + N_CHUNK] = z
    wo_out_ref[...] = wo_ref[...].astype(bf16)
    w1_out_ref[...] = w1_ref[...].astype(bf16)
    w2_out_ref[...] = w2_ref[...].astype(bf16)


def _in_proj(x2d, gains, w_in_bf16, w_out, w_ff_in, w_ff_out, layer, batch, seq):
    t = x2d.shape[0]
    steps = t // IN_ROW_TILE
    tiles_per_seq = seq // IN_ROW_TILE
    rows_in = lambda w: pl.BlockSpec((None, w.shape[1] // steps, w.shape[2]),
                                     lambda i: (layer, i, 0))
    rows_out = lambda w: pl.BlockSpec((w.shape[1] // steps, w.shape[2]), lambda i: (i, 0))
    cast = (w_out, w_ff_in, w_ff_out)
    for w in cast:
        assert w.shape[1] % (steps * 2 * SUBLANES) == 0
    return pl.pallas_call(
        _in_proj_kernel,
        grid=(steps,),
        in_specs=[
            pl.BlockSpec((IN_ROW_TILE, D_MODEL), lambda i: (i, 0)),
            _layer_vec(D_MODEL, layer),
            pl.BlockSpec((None, D_MODEL, IN_WIDTH), lambda i: (layer, 0, 0),
                         pipeline_mode=pl.Buffered(1)),
        ] + [rows_in(w) for w in cast],
        out_specs=[
            pl.BlockSpec((1, QKV_SLABS, IN_ROW_TILE, LANES),
                         lambda i: (i // tiles_per_seq, 0, i % tiles_per_seq, 0)),
            pl.BlockSpec((IN_ROW_TILE, LRU_WIDTH), lambda i: (i, 0)),
            pl.BlockSpec((IN_ROW_TILE, LRU_WIDTH), lambda i: (i, 0)),
        ] + [rows_out(w) for w in cast],
        out_shape=[
            jax.ShapeDtypeStruct((batch, QKV_SLABS, seq, LANES), f32),
            jax.ShapeDtypeStruct((t, LRU_WIDTH), f32),
            jax.ShapeDtypeStruct((t, LRU_WIDTH), f32),
        ] + [jax.ShapeDtypeStruct(w.shape[1:], bf16) for w in cast],
        compiler_params=pltpu.CompilerParams(
            dimension_semantics=("parallel",), vmem_limit_bytes=VMEM_LIMIT),
        name="in_proj",
    )(x2d, gains, w_in_bf16, *cast)


def _rows(start, size, stride=1):
    if stride == 1:
        if not isinstance(start, int):
            start = pl.multiple_of(start, SPAN)
        return pl.ds(start, size)
    return pl.ds(start, size, stride=stride)


def _attn_kernel(q_ref, k_ref, v_ref, o_ref, num_ref, den_ref, mx_ref, q4_ref, k4_ref, v4_ref,
                 band_ref, first_ref):
    seq = q_ref.shape[2]
    slab_rows = seq // SLABS
    lane = lax.broadcasted_iota(jnp.int32, (1, LANES), 1)
    head_sel = (lane < HEAD_DIM, lane >= HEAD_DIM)
    nt = (((1,), (1,)), ((), ()))

    def ones_sel(h, rows):
        keep = lax.broadcasted_iota(jnp.int32, (rows, LANES), 1) < HEAD_DIM
        return jnp.where(keep if h == 0 else ~keep, 1.0, 0.0).astype(bf16)

    dist = (SPAN + lax.broadcasted_iota(jnp.int32, (SPAN, 2 * SPAN), 0)
            - lax.broadcasted_iota(jnp.int32, (SPAN, 2 * SPAN), 1))
    band_ref[...] = jnp.where((dist >= 0) & (dist <= SPAN), 0.0, -jnp.inf)
    first_ref[...] = jnp.where(lax.broadcasted_iota(jnp.int32, (SPAN, SPAN), 0)
                               >= lax.broadcasted_iota(jnp.int32, (SPAN, SPAN), 1), 0.0, -jnp.inf)

    def make_units(qall, kall, vall, count, has_first, sinks):
        qall = (qall * Q_SCALE).astype(bf16)
        kall, vall = kall.astype(bf16), vall.astype(bf16)
        units = []
        for u in range(count):
            q = qall[u * SPAN:(u + 1) * SPAN]
            if has_first and u == 0:
                k, v, bias = kall[:SPAN], vall[:SPAN], first_ref
            else:
                i = u if has_first else u + 1
                k, v = kall[(i - 1) * SPAN:(i + 1) * SPAN], vall[(i - 1) * SPAN:(i + 1) * SPAN]
                bias = band_ref
            units.append((q, k, v, bias, sinks[u]))
        return units

    def run(units):
        scores = {}

        def issue_qk(i):
            q, k, _, bias, _ = units[i]
            scores[i] = [lax.dot_general(q, jnp.where(sel, k, jnp.zeros_like(k)), nt,
                                         preferred_element_type=f32) + bias[...]
                         for sel in head_sel]

        def finish(i):
            _, _, v, _, (p, rows) = units[i]
            probs, maxes = [], []
            for s in scores.pop(i):
                mx = jnp.max(s, axis=-1, keepdims=True)
                probs.append(jnp.exp2(s - mx).astype(bf16))
                maxes.append(mx)
            vaug = jnp.concatenate(
                [jnp.concatenate([jnp.where(sel, v, jnp.zeros_like(v)), ones_sel(h, v.shape[0])],
                                 axis=1) for h, sel in enumerate(head_sel)], axis=0)
            acc = jnp.dot(jnp.concatenate(probs, axis=1), vaug, preferred_element_type=f32)
            num_ref[p, rows, :] = acc[:, :LANES]
            den_ref[p, rows, :] = acc[:, LANES:]
            mx_ref[p, rows, :] = jnp.where(head_sel[0], maxes[0], maxes[1])

        for i in range(min(QK_LOOKAHEAD, len(units))):
            issue_qk(i)
        for i in range(len(units)):
            if i + QK_LOOKAHEAD < len(units):
                issue_qk(i + QK_LOOKAHEAD)
            finish(i)

    def d1_group(n0, has_first):
        lo = n0 if has_first else n0 - 1
        nk = ATTN_GROUP if has_first else ATTN_GROUP + 1
        sinks = [(0, _rows((n0 + u) * SPAN, SPAN)) for u in range(ATTN_GROUP)]
        run(make_units(q_ref[0, 0, _rows(n0 * SPAN, ATTN_GROUP * SPAN), :],
                       k_ref[0, 0, _rows(lo * SPAN, nk * SPAN), :],
                       v_ref[0, 0, _rows(lo * SPAN, nk * SPAN), :],
                       ATTN_GROUP, has_first, sinks))

    d1_group(0, True)

    def d1_step(g, carry):
        d1_group(g * ATTN_GROUP, False)
        return carry

    lax.fori_loop(1, seq // (SPAN * ATTN_GROUP), d1_step, 0)

    blocks4 = slab_rows // SPAN
    classes4 = ATTN_GROUP // blocks4

    def d4_step(g, carry):
        units = []
        for cc in range(classes4):
            c = g * classes4 + cc
            qall = q_ref[0, 0, _rows(c, slab_rows, SLABS), :]
            kall = k_ref[0, 0, _rows(c, slab_rows, SLABS), :]
            vall = v_ref[0, 0, _rows(c, slab_rows, SLABS), :]
            q4_ref[c], k4_ref[c], v4_ref[c] = qall, kall, vall
            sinks = [(1, _rows(c * slab_rows + u * SPAN, SPAN)) for u in range(blocks4)]
            units += make_units(qall, kall, vall, blocks4, True, sinks)
        run(units)
        return carry

    lax.fori_loop(0, SLABS // classes4, d4_step, 0)

    blocks16 = slab_rows // (WALK * SPAN)
    classes16 = ATTN_GROUP // blocks16

    def d16_step(g, carry):
        units = []
        for cc in range(classes16):
            c16 = g * classes16 + cc
            c4, j = c16 % SLABS, c16 // SLABS
            rows = _rows(j, blocks16 * SPAN, WALK)
            sinks = [(2, _rows(c4 * slab_rows + j + WALK * SPAN * u, SPAN, WALK))
                     for u in range(blocks16)]
            units += make_units(q4_ref[c4, rows, :], k4_ref[c4, rows, :], v4_ref[c4, rows, :],
                                blocks16, True, sinks)
        run(units)
        return carry

    lax.fori_loop(0, MAX_DILATION // classes16, d16_step, 0)

    merge_rows = min(MERGE_ROWS, slab_rows)
    per_slab = slab_rows // merge_rows

    def merge(i, carry):
        c4 = i // per_slab
        m0 = (i - c4 * per_slab) * merge_rows
        rows = _rows(c4 * slab_rows + m0, merge_rows)
        tok = c4 + SLABS * m0
        trows = _rows(tok, merge_rows, SLABS)

        def stat(ref, p):
            return ref[p, trows if p == 0 else rows, :]

        mx = [stat(mx_ref, p) for p in range(3)]
        top = jnp.maximum(jnp.maximum(mx[0], mx[1]), mx[2])
        w = [jnp.exp2(m - top) for m in mx]
        num = sum(w[p] * stat(num_ref, p) for p in range(3))
        den = sum(w[p] * stat(den_ref, p) for p in range(3))
        o_ref[0, 0, trows, :] = num / den
        return carry

    lax.fori_loop(0, SLABS * per_slab, merge, 0)


def _attention(qkv):
    b, _, s, _ = qkv.shape
    assert s % (MAX_DILATION * SPAN) == 0 and (s // SPAN) % ATTN_GROUP == 0
    blocks4 = s // (SLABS * SPAN)
    assert ATTN_GROUP % blocks4 == 0 and SLABS % (ATTN_GROUP // blocks4) == 0
    blk = (1, 1, s, LANES)
    stat = pltpu.VMEM((len(DILATIONS), s, LANES), f32)
    mod4 = pltpu.VMEM((SLABS, s // SLABS, LANES), f32)
    return pl.pallas_call(
        _attn_kernel,
        grid=(b, HEAD_PAIRS),
        in_specs=[
            pl.BlockSpec(blk, lambda i, j: (i, j, 0, 0)),
            pl.BlockSpec(blk, lambda i, j: (i, HEAD_PAIRS + j, 0, 0)),
            pl.BlockSpec(blk, lambda i, j: (i, 2 * HEAD_PAIRS + j, 0, 0)),
        ],
        out_specs=pl.BlockSpec((1, 1, s, LANES), lambda i, j: (i, j, 0, 0)),
        out_shape=jax.ShapeDtypeStruct((b, HEAD_PAIRS, s, LANES), f32),
        scratch_shapes=[
            stat, stat, stat, mod4, mod4, mod4,
            pltpu.VMEM((SPAN, 2 * SPAN), f32),
            pltpu.VMEM((SPAN, SPAN), f32),
        ],
        compiler_params=pltpu.CompilerParams(
            dimension_semantics=("parallel", "parallel"), vmem_limit_bytes=VMEM_LIMIT),
        name="dilated_attention",
    )(qkv, qkv, qkv)


def _sigmoid_of_twice(half_x):
    return 0.5 * jnp.tanh(half_x) + 0.5


def _rglru_kernel(xr_ref, gr_ref, cw_ref, cb_ref, wg_ref, br_ref, bi_ref, lam_ref,
                  o_ref, xi_ref, a_ref, u_ref, h_ref, *, batch):
    tc = LRU_TIME_CHUNK
    halo = (CONV_WIDTH - 1) * batch
    step = pl.program_id(0)

    @pl.when(step == 0)
    def _():
        h_ref[...] = jnp.zeros_like(h_ref)
        xi_ref[:, 0:halo, :] = jnp.zeros((LRU_GROUPS, halo, LANES), f32)

    def interleave(b, carry):
        x = xr_ref[b]
        for g in range(LRU_GROUPS):
            xi_ref[g, pl.ds(halo + b, tc, stride=batch), :] = x[:, g * LANES:(g + 1) * LANES]
        return carry

    lax.fori_loop(0, batch, interleave, 0, unroll=True)

    neg_lam = -lam_ref[...]
    softplus = jnp.maximum(neg_lam, 0.0) + jnp.log1p(jnp.exp(-jnp.abs(neg_lam)))
    rate = -LRU_C * softplus
    half = LRU_WIDTH // 2
    chunk = LRU_DENSE_ROWS

    def dense(ci, carry):
        r0 = pl.multiple_of(ci * chunk, chunk)
        y = cb_ref[...]
        for j in range(CONV_WIDTH):
            tap = CONV_WIDTH - 1 - j
            rows = pl.ds(pl.multiple_of(r0 + halo - j * batch, batch), chunk)
            xs = jnp.concatenate([xi_ref[g, rows, :] for g in range(LRU_GROUPS)], axis=-1)
            y = y + cw_ref[tap:tap + 1, :] * xs
        yb = y.astype(bf16)
        gates = [jnp.dot(yb[:, hf * half:(hf + 1) * half], wg_ref[hf],
                         preferred_element_type=f32) for hf in range(2)]
        pre_r = jnp.concatenate([gates[0][:, :half], gates[1][:, :half]], axis=-1)
        pre_i = jnp.concatenate([gates[0][:, half:], gates[1][:, half:]], axis=-1)
        r = _sigmoid_of_twice(pre_r + br_ref[...])
        i = _sigmoid_of_twice(pre_i + bi_ref[...])
        log_a = rate * r
        a = jnp.exp(log_a)
        gain_sq = -jnp.tanh(log_a) * (a * a + 1.0)
        u = (gain_sq * lax.rsqrt(jnp.maximum(gain_sq, SQRT_FLOOR))) * (i * y)
        for g in range(LRU_GROUPS):
            cols = slice(g * LANES, (g + 1) * LANES)
            a_ref[g, pl.ds(r0, chunk), :] = a[:, cols]
            u_ref[g, pl.ds(r0, chunk), :] = u[:, cols]
        return carry

    lax.fori_loop(0, tc * batch // chunk, dense, 0)
    xi_ref[:, 0:halo, :] = xi_ref[:, tc * batch:tc * batch + halo, :]

    def scan(t, hs):
        rows = pl.ds(pl.multiple_of(t * batch, batch), batch)
        new = []
        for g in range(LRU_GROUPS):
            h = a_ref[g, rows, :] * hs[g] + u_ref[g, rows, :]
            u_ref[g, rows, :] = h
            new.append(h)
        return tuple(new)

    hs = lax.fori_loop(0, tc, scan, tuple(h_ref[g] for g in range(LRU_GROUPS)),
                       unroll=LRU_SCAN_UNROLL)
    for g in range(LRU_GROUPS):
        h_ref[g] = hs[g]

    def emit(b, carry):
        h = jnp.concatenate(
            [u_ref[g, pl.ds(b, tc, stride=batch), :] for g in range(LRU_GROUPS)], axis=-1)
        o_ref[b] = (h * jax.nn.gelu(gr_ref[b])).astype(o_ref.dtype)
        return carry

    lax.fori_loop(0, batch, emit, 0, unroll=True)


def _rglru(xr3, gr3, conv_w, conv_b, w_gates, b_r, b_i, lam, layer):
    b, s, _ = xr3.shape
    assert b == SUBLANES
    tc = LRU_TIME_CHUNK
    blk = (b, tc, LRU_WIDTH)
    vec = _layer_vec(LRU_WIDTH, layer)
    return pl.pallas_call(
        functools.partial(_rglru_kernel, batch=b),
        grid=(s // tc,),
        in_specs=[
            pl.BlockSpec(blk, lambda i: (0, i, 0)),
            pl.BlockSpec(blk, lambda i: (0, i, 0)),
            pl.BlockSpec((None, CONV_WIDTH, LRU_WIDTH), lambda i: (layer, 0, 0)),
            vec,
            pl.BlockSpec((None, 2, LRU_WIDTH // 2, LRU_WIDTH), lambda i: (layer, 0, 0, 0)),
            vec, vec, vec,
        ],
        out_specs=pl.BlockSpec(blk, lambda i: (0, i, 0)),
        out_shape=jax.ShapeDtypeStruct((b, s, LRU_WIDTH), bf16),
        scratch_shapes=[
            pltpu.VMEM((LRU_GROUPS, (tc + CONV_WIDTH - 1) * b, LANES), f32),
            pltpu.VMEM((LRU_GROUPS, tc * b, LANES), f32),
            pltpu.VMEM((LRU_GROUPS, tc * b, LANES), f32),
            pltpu.VMEM((LRU_GROUPS, b, LANES), f32),
        ],
        compiler_params=pltpu.CompilerParams(
            dimension_semantics=("arbitrary",), vmem_limit_bytes=VMEM_LIMIT),
        name="rglru",
    )(xr3, gr3, conv_w, conv_b, w_gates, b_r, b_i, lam)


def _gate_weights(w_r, w_i):
    per_half = N_LRU_BLOCKS // 2
    eye = jnp.eye(per_half, dtype=w_r.dtype)

    def dense(w):
        d = w.shape[0]
        blocks = w.reshape(d, 2, per_half, LRU_BLOCK, LRU_BLOCK)
        full = jnp.einsum('dhaij,ab->dhaibj', blocks, eye)
        return full.reshape(d, 2, per_half * LRU_BLOCK, per_half * LRU_BLOCK)

    return (jnp.concatenate([dense(w_r), dense(w_i)], axis=-1) * GATE_PRESCALE).astype(bf16)


def _out_mlp_kernel(*refs):
    attn_refs = refs[:HEAD_PAIRS]
    rec_ref, x_ref, gpost_ref, gpre2_ref, gpost2_ref, wo_ref, w1_ref, w2_ref, o_ref = refs[HEAD_PAIRS:]
    subs = [slice(i * MLP_SUB_ROWS, (i + 1) * MLP_SUB_ROWS) for i in range(ROW_TILE // MLP_SUB_ROWS)]
    ff_chunk = 2 * N_CHUNK

    def out_proj(r):
        mix = jnp.concatenate([a[r, :].astype(bf16) for a in attn_refs] + [rec_ref[r, :]], axis=-1)
        return jnp.dot(mix, wo_ref[...], preferred_element_type=f32)

    m_next = out_proj(subs[0])
    for i, r in enumerate(subs):
        m = m_next
        if i + 1 < len(subs):
            m_next = out_proj(subs[i + 1])
        x1 = x_ref[r, :] + _rms(m, gpost_ref[...])
        h = _rms(x1, gpre2_ref[...]).astype(bf16)
        acc = None
        for n in range(D_FF // ff_chunk):
            cols = slice(n * ff_chunk, (n + 1) * ff_chunk)
            a = jnp.maximum(jnp.dot(h, w1_ref[:, cols], preferred_element_type=f32), 0.0)
            part = jnp.dot((a * a).astype(bf16), w2_ref[cols, :], preferred_element_type=f32)
            acc = part if acc is None else acc + part
        o_ref[r, :] = x1 + _rms(acc, gpost2_ref[...])


def _out_mlp(attn, rec2d, x2d, g_post, g_pre2, g_post2, wo, w1, w2, layer):
    t = x2d.shape[0]
    tiles_per_seq = attn.shape[2] // ROW_TILE
    vec = _layer_vec(D_MODEL, layer)
    const = lambda shape: pl.BlockSpec(shape, lambda i: (0, 0), pipeline_mode=pl.Buffered(1))
    return pl.pallas_call(
        _out_mlp_kernel,
        grid=(t // ROW_TILE,),
        in_specs=[
            pl.BlockSpec((None, None, ROW_TILE, LANES),
                         lambda i, hp=hp: (i // tiles_per_seq, hp, i % tiles_per_seq, 0))
            for hp in range(HEAD_PAIRS)
        ] + [
            pl.BlockSpec((ROW_TILE, LRU_WIDTH), lambda i: (i, 0)),
            pl.BlockSpec((ROW_TILE, D_MODEL), lambda i: (i, 0)),
            vec, vec, vec,
            const((ATTN_WIDTH + LRU_WIDTH, D_MODEL)),
            const((D_MODEL, D_FF)),
            const((D_FF, D_MODEL)),
        ],
        out_specs=pl.BlockSpec((ROW_TILE, D_MODEL), lambda i: (i, 0)),
        out_shape=jax.ShapeDtypeStruct((t, D_MODEL), f32),
        compiler_params=pltpu.CompilerParams(
            dimension_semantics=("parallel",), vmem_limit_bytes=VMEM_LIMIT),
        name="out_mlp",
    )(*([attn] * HEAD_PAIRS), rec2d, x2d, g_post, g_pre2, g_post2, wo, w1, w2)


def kernel(x, mix_norm_pre, mix_norm_post, mlp_norm_pre, mlp_norm_post, w_in, conv_w, conv_b,
           w_rgate, b_rgate, w_igate, b_igate, lru_lambda, w_out, w_ff_in, w_ff_out):
    b, s, d = x.shape
    depth = w_in.shape[0]
    t = b * s
    x2d = x.reshape(t, d)
    rows = lambda v: v.reshape(depth, 1, -1)
    w_in_bf16 = w_in.astype(bf16)
    w_gates = _gate_weights(w_rgate, w_igate)
    b_r, b_i = rows(b_rgate) * GATE_PRESCALE, rows(b_igate) * GATE_PRESCALE
    for l in range(depth):
        qkv, xr, gr, wo, w1, w2 = _in_proj(x2d, rows(mix_norm_pre), w_in_bf16, w_out, w_ff_in,
                                           w_ff_out, l, b, s)
        attn = _attention(qkv)
        rec = _rglru(xr.reshape(b, s, LRU_WIDTH), gr.reshape(b, s, LRU_WIDTH), conv_w,
                     rows(conv_b), w_gates, b_r, b_i, rows(lru_lambda), l)
        x2d = _out_mlp(attn, rec.reshape(t, LRU_WIDTH), x2d,
                       rows(mix_norm_post), rows(mlp_norm_pre), rows(mlp_norm_post), wo, w1, w2, l)
    return x2d.reshape(b, s, d)
```

```python
import functools
import math

import jax
import jax.numpy as jnp
from jax import lax
from jax.experimental import pallas as pl
from jax.experimental.pallas import tpu as pltpu

D_MODEL = 1024
N_HEADS = 8
HEAD_DIM = 64
ATTN_WIDTH = N_HEADS * HEAD_DIM
DILATIONS = (1, 4, 16)
MAX_DILATION = 16
SPAN = 128
LRU_WIDTH = 512
N_LRU_BLOCKS = 8
LRU_BLOCK = LRU_WIDTH // N_LRU_BLOCKS
CONV_WIDTH = 4
LRU_C = 8.0
IN_WIDTH = 3 * ATTN_WIDTH + 2 * LRU_WIDTH
D_FF = 4 * D_MODEL
NORM_EPS = 1e-6

LANES = 128
SUBLANES = 8
VMEM_LIMIT = 56 * 1024 * 1024

ROW_TILE = 512
IN_ROW_TILE = 1024
N_CHUNK = 512
LRU_TIME_CHUNK = 256
LRU_DENSE_ROWS = 2048
LRU_SCAN_UNROLL = 256
SLABS = DILATIONS[1]
WALK = MAX_DILATION // SLABS
MERGE_ROWS = 1024
HEAD_PAIRS = ATTN_WIDTH // LANES
QKV_SLABS = 3 * HEAD_PAIRS
LRU_GROUPS = LRU_WIDTH // LANES
ATTN_GROUP = 32
QK_LOOKAHEAD = 4
GATE_PRESCALE = 0.5
SQRT_FLOOR = 1e-37
Q_SCALE = HEAD_DIM ** -0.5 * math.log2(math.e)

bf16 = jnp.bfloat16
f32 = jnp.float32


def _rms(x, g):
    y = x * lax.rsqrt(jnp.mean(x * x, axis=-1, keepdims=True) + NORM_EPS)
    return y * g


def _layer_vec(width, layer):
    return pl.BlockSpec((None, 1, width), lambda i: (layer, 0, 0))


def _in_proj_kernel(x_ref, g_ref, w_ref, wo_ref, w1_ref, w2_ref,
                    qkv_ref, xr_ref, gr_ref, wo_out_ref, w1_out_ref, w2_out_ref):
    h = _rms(x_ref[...], g_ref[...]).astype(bf16)
    slabs_per_chunk = N_CHUNK // LANES
    qkv_chunks = QKV_SLABS // slabs_per_chunk
    for n in range(IN_WIDTH // N_CHUNK):
        z = jnp.dot(h, w_ref[:, n * N_CHUNK:(n + 1) * N_CHUNK], preferred_element_type=f32)
        if n < qkv_chunks:
            for s in range(slabs_per_chunk):
                qkv_ref[0, n * slabs_per_chunk + s] = z[:, s * LANES:(s + 1) * LANES]
        else:
            out_ref = (xr_ref, gr_ref)[(n - qkv_chunks) * N_CHUNK // LRU_WIDTH]
            col = (n - qkv_chunks) * N_CHUNK % LRU_WIDTH
            out_ref[:, col:col + N_CHUNK] = z
    wo_out_ref[...] = wo_ref[...].astype(bf16)
    w1_out_ref[...] = w1_ref[...].astype(bf16)
    w2_out_ref[...] = w2_ref[...].astype(bf16)


def _in_proj(x2d, gains, w_in_bf16, w_out, w_ff_in, w_ff_out, layer, batch, seq):
    t = x2d.shape[0]
    steps = t // IN_ROW_TILE
    tiles_per_seq = seq // IN_ROW_TILE
    rows_in = lambda w: pl.BlockSpec((None, w.shape[1] // steps, w.shape[2]),
                                     lambda i: (layer, i, 0))
    rows_out = lambda w: pl.BlockSpec((w.shape[1] // steps, w.shape[2]), lambda i: (i, 0))
    cast = (w_out, w_ff_in, w_ff_out)
    for w in cast:
        assert w.shape[1] % (steps * 2 * SUBLANES) == 0
    return pl.pallas_call(
        _in_proj_kernel,
        grid=(steps,),
        in_specs=[
            pl.BlockSpec((IN_ROW_TILE, D_MODEL), lambda i: (i, 0)),
            _layer_vec(D_MODEL, layer),
            pl.BlockSpec((None, D_MODEL, IN_WIDTH), lambda i: (layer, 0, 0),
                         pipeline_mode=pl.Buffered(1)),
        ] + [rows_in(w) for w in cast],
        out_specs=[
            pl.BlockSpec((1, QKV_SLABS, IN_ROW_TILE, LANES),
                         lambda i: (i // tiles_per_seq, 0, i % tiles_per_seq, 0)),
            pl.BlockSpec((IN_ROW_TILE, LRU_WIDTH), lambda i: (i, 0)),
            pl.BlockSpec((IN_ROW_TILE, LRU_WIDTH), lambda i: (i, 0)),
        ] + [rows_out(w) for w in cast],
        out_shape=[
            jax.ShapeDtypeStruct((batch, QKV_SLABS, seq, LANES), f32),
            jax.ShapeDtypeStruct((t, LRU_WIDTH), f32),
            jax.ShapeDtypeStruct((t, LRU_WIDTH), f32),
        ] + [jax.ShapeDtypeStruct(w.shape[1:], bf16) for w in cast],
        compiler_params=pltpu.CompilerParams(
            dimension_semantics=("parallel",), vmem_limit_bytes=VMEM_LIMIT),
        name="in_proj",
    )(x2d, gains, w_in_bf16, *cast)


def _rows(start, size, stride=1):
    if stride == 1:
        if not isinstance(start, int):
            start = pl.multiple_of(start, SPAN)
        return pl.ds(start, size)
    return pl.ds(start, size, stride=stride)


def _attn_kernel(q_ref, k_ref, v_ref, o_ref, num_ref, den_ref, mx_ref, q4_ref, k4_ref, v4_ref,
                 band_ref, first_ref):
    seq = q_ref.shape[2]
    slab_rows = seq // SLABS
    lane = lax.broadcasted_iota(jnp.int32, (1, LANES), 1)
    head_sel = (lane < HEAD_DIM, lane >= HEAD_DIM)
    nt = (((1,), (1,)), ((), ()))

    def ones_sel(h, rows):
        keep = lax.broadcasted_iota(jnp.int32, (rows, LANES), 1) < HEAD_DIM
        return jnp.where(keep if h == 0 else ~keep, 1.0, 0.0).astype(bf16)

    dist = (SPAN + lax.broadcasted_iota(jnp.int32, (SPAN, 2 * SPAN), 0)
            - lax.broadcasted_iota(jnp.int32, (SPAN, 2 * SPAN), 1))
    band_ref[...] = jnp.where((dist >= 0) & (dist <= SPAN), 0.0, -jnp.inf)
    first_ref[...] = jnp.where(lax.broadcasted_iota(jnp.int32, (SPAN, SPAN), 0)
                               >= lax.broadcasted_iota(jnp.int32, (SPAN, SPAN), 1), 0.0, -jnp.inf)

    def make_units(qall, kall, vall, count, has_first, sinks):
        qall = (qall * Q_SCALE).astype(bf16)
        kall, vall = kall.astype(bf16), vall.astype(bf16)
        units = []
        for u in range(count):
            q = qall[u * SPAN:(u + 1) * SPAN]
            if has_first and u == 0:
                k, v, bias = kall[:SPAN], vall[:SPAN], first_ref
            else:
                i = u if has_first else u + 1
                k, v = kall[(i - 1) * SPAN:(i + 1) * SPAN], vall[(i - 1) * SPAN:(i + 1) * SPAN]
                bias = band_ref
            units.append((q, k, v, bias, sinks[u]))
        return units

    def run(units):
        scores = {}

        def issue_qk(i):
            q, k, _, bias, _ = units[i]
            scores[i] = [lax.dot_general(q, jnp.where(sel, k, jnp.zeros_like(k)), nt,
                                         preferred_element_type=f32) + bias[...]
                         for sel in head_sel]

        def finish(i):
            _, _, v, _, (p, rows) = units[i]
            probs, maxes = [], []
            for s in scores.pop(i):
                mx = jnp.max(s, axis=-1, keepdims=True)
                probs.append(jnp.exp2(s - mx).astype(bf16))
                maxes.append(mx)
            vaug = jnp.concatenate(
                [jnp.concatenate([jnp.where(sel, v, jnp.zeros_like(v)), ones_sel(h, v.shape[0])],
                                 axis=1) for h, sel in enumerate(head_sel)], axis=0)
            acc = jnp.dot(jnp.concatenate(probs, axis=1), vaug, preferred_element_type=f32)
            num_ref[p, rows, :] = acc[:, :LANES]
            den_ref[p, rows, :] = acc[:, LANES:]
            mx_ref[p, rows, :] = jnp.where(head_sel[0], maxes[0], maxes[1])

        for i in range(min(QK_LOOKAHEAD, len(units))):
            issue_qk(i)
        for i in range(len(units)):
            if i + QK_LOOKAHEAD < len(units):
                issue_qk(i + QK_LOOKAHEAD)
            finish(i)

    def d1_group(n0, has_first):
        lo = n0 if has_first else n0 - 1
        nk = ATTN_GROUP if has_first else ATTN_GROUP + 1
        sinks = [(0, _rows((n0 + u) * SPAN, SPAN)) for u in range(ATTN_GROUP)]
        run(make_units(q_ref[0, 0, _rows(n0 * SPAN, ATTN_GROUP * SPAN), :],
                       k_ref[0, 0, _rows(lo * SPAN, nk * SPAN), :],
                       v_ref[0, 0, _rows(lo * SPAN, nk * SPAN), :],
                       ATTN_GROUP, has_first, sinks))

    d1_group(0, True)

    def d1_step(g, carry):
        d1_group(g * ATTN_GROUP, False)
        return carry

    lax.fori_loop(1, seq // (SPAN * ATTN_GROUP), d1_step, 0)

    blocks4 = slab_rows // SPAN
    classes4 = ATTN_GROUP // blocks4

    def d4_step(g, carry):
        units = []
        for cc in range(classes4):
            c = g * classes4 + cc
            qall = q_ref[0, 0, _rows(c, slab_rows, SLABS), :]
            kall = k_ref[0, 0, _rows(c, slab_rows, SLABS), :]
            vall = v_ref[0, 0, _rows(c, slab_rows, SLABS), :]
            q4_ref[c], k4_ref[c], v4_ref[c] = qall, kall, vall
            sinks = [(1, _rows(c * slab_rows + u * SPAN, SPAN)) for u in range(blocks4)]
            units += make_units(qall, kall, vall, blocks4, True, sinks)
        run(units)
        return carry

    lax.fori_loop(0, SLABS // classes4, d4_step, 0)

    blocks16 = slab_rows // (WALK * SPAN)
    classes16 = ATTN_GROUP // blocks16

    def d16_step(g, carry):
        units = []
        for cc in range(classes16):
            c16 = g * classes16 + cc
            c4, j = c16 % SLABS, c16 // SLABS
            rows = _rows(j, blocks16 * SPAN, WALK)
            sinks = [(2, _rows(c4 * slab_rows + j + WALK * SPAN * u, SPAN, WALK))
                     for u in range(blocks16)]
            units += make_units(q4_ref[c4, rows, :], k4_ref[c4, rows, :], v4_ref[c4, rows, :],
                                blocks16, True, sinks)
        run(units)
        return carry

    lax.fori_loop(0, MAX_DILATION // classes16, d16_step, 0)

    merge_rows = min(MERGE_ROWS, slab_rows)
    per_slab = slab_rows // merge_rows

    def merge(i, carry):
        c4 = i // per_slab
        m0 = (i - c4 * per_slab) * merge_rows
        rows = _rows(c4 * slab_rows + m0, merge_rows)
        tok = c4 + SLABS * m0
        trows = _rows(tok, merge_rows, SLABS)

        def stat(ref, p):
            return ref[p, trows if p == 0 else rows, :]

        mx = [stat(mx_ref, p) for p in range(3)]
        top = jnp.maximum(jnp.maximum(mx[0], mx[1]), mx[2])
        w = [jnp.exp2(m - top) for m in mx]
        num = sum(w[p] * stat(num_ref, p) for p in range(3))
        den = sum(w[p] * stat(den_ref, p) for p in range(3))
        o_ref[0, 0, trows, :] = num / den
        return carry

    lax.fori_loop(0, SLABS * per_slab, merge, 0, unroll=2)


def _attention(qkv):
    b, _, s, _ = qkv.shape
    assert s % (MAX_DILATION * SPAN) == 0 and (s // SPAN) % ATTN_GROUP == 0
    blocks4 = s // (SLABS * SPAN)
    assert ATTN_GROUP % blocks4 == 0 and SLABS % (ATTN_GROUP // blocks4) == 0
    blk = (1, 1, s, LANES)
    stat = pltpu.VMEM((len(DILATIONS), s, LANES), f32)
    mod4 = pltpu.VMEM((SLABS, s // SLABS, LANES), f32)
    return pl.pallas_call(
        _attn_kernel,
        grid=(b, HEAD_PAIRS),
        in_specs=[
            pl.BlockSpec(blk, lambda i, j: (i, j, 0, 0)),
            pl.BlockSpec(blk, lambda i, j: (i, HEAD_PAIRS + j, 0, 0)),
            pl.BlockSpec(blk, lambda i, j: (i, 2 * HEAD_PAIRS + j, 0, 0)),
        ],
        out_specs=pl.BlockSpec((1, 1, s, LANES), lambda i, j: (i, j, 0, 0)),
        out_shape=jax.ShapeDtypeStruct((b, HEAD_PAIRS, s, LANES), f32),
        scratch_shapes=[
            stat, stat, stat, mod4, mod4, mod4,
            pltpu.VMEM((SPAN, 2 * SPAN), f32),
            pltpu.VMEM((SPAN, SPAN), f32),
        ],
        compiler_params=pltpu.CompilerParams(
            dimension_semantics=("parallel", "parallel"), vmem_limit_bytes=VMEM_LIMIT),
        name="dilated_attention",
    )(qkv, qkv, qkv)


def _sigmoid_of_twice(half_x):
    return 0.5 * jnp.tanh(half_x) + 0.5


def _rglru_kernel(xr_ref, gr_ref, cw_ref, cb_ref, wg_ref, br_ref, bi_ref, lam_ref,
                  o_ref, xi_ref, a_ref, u_ref, h_ref, *, batch):
    tc = LRU_TIME_CHUNK
    halo = (CONV_WIDTH - 1) * batch
    step = pl.program_id(0)

    @pl.when(step == 0)
    def _():
        h_ref[...] = jnp.zeros_like(h_ref)
        xi_ref[:, 0:halo, :] = jnp.zeros((LRU_GROUPS, halo, LANES), f32)

    def interleave(b, carry):
        x = xr_ref[b]
        for g in range(LRU_GROUPS):
            xi_ref[g, pl.ds(halo + b, tc, stride=batch), :] = x[:, g * LANES:(g + 1) * LANES]
        return carry

    lax.fori_loop(0, batch, interleave, 0, unroll=True)

    neg_lam = -lam_ref[...]
    softplus = jnp.maximum(neg_lam, 0.0) + jnp.log1p(jnp.exp(-jnp.abs(neg_lam)))
    rate = -LRU_C * softplus
    half = LRU_WIDTH // 2
    chunk = LRU_DENSE_ROWS

    def dense(ci, carry):
        r0 = pl.multiple_of(ci * chunk, chunk)
        y = cb_ref[...]
        for j in range(CONV_WIDTH):
            tap = CONV_WIDTH - 1 - j
            rows = pl.ds(pl.multiple_of(r0 + halo - j * batch, batch), chunk)
            xs = jnp.concatenate([xi_ref[g, rows, :] for g in range(LRU_GROUPS)], axis=-1)
            y = y + cw_ref[tap:tap + 1, :] * xs
        yb = y.astype(bf16)
        gates = [jnp.dot(yb[:, hf * half:(hf + 1) * half], wg_ref[hf],
                         preferred_element_type=f32) for hf in range(2)]
        pre_r = jnp.concatenate([gates[0][:, :half], gates[1][:, :half]], axis=-1)
        pre_i = jnp.concatenate([gates[0][:, half:], gates[1][:, half:]], axis=-1)
        r = _sigmoid_of_twice(pre_r + br_ref[...])
        i = _sigmoid_of_twice(pre_i + bi_ref[...])
        log_a = rate * r
        a = jnp.exp(log_a)
        gain_sq = -jnp.tanh(log_a) * (a * a + 1.0)
        u = (gain_sq * lax.rsqrt(jnp.maximum(gain_sq, SQRT_FLOOR))) * (i * y)
        for g in range(LRU_GROUPS):
            cols = slice(g * LANES, (g + 1) * LANES)
            a_ref[g, pl.ds(r0, chunk), :] = a[:, cols]
            u_ref[g, pl.ds(r0, chunk), :] = u[:, cols]
        return carry

    lax.fori_loop(0, tc * batch // chunk, dense, 0)
    xi_ref[:, 0:halo, :] = xi_ref[:, tc * batch:tc * batch + halo, :]

    def scan(t, hs):
        rows = pl.ds(pl.multiple_of(t * batch, batch), batch)
        new = []
        for g in range(LRU_GROUPS):
            h = a_ref[g, rows, :] * hs[g] + u_ref[g, rows, :]
            u_ref[g, rows, :] = h
            new.append(h)
        return tuple(new)

    hs = lax.fori_loop(0, tc, scan, tuple(h_ref[g] for g in range(LRU_GROUPS)),
                       unroll=LRU_SCAN_UNROLL)
    for g in range(LRU_GROUPS):
        h_ref[g] = hs[g]

    def emit(b, carry):
        h = jnp.concatenate(
            [u_ref[g, pl.ds(b, tc, stride=batch), :] for g in range(LRU_GROUPS)], axis=-1)
        o_ref[b] = (h * jax.nn.gelu(gr_ref[b])).astype(o_ref.dtype)
        return carry

    lax.fori_loop(0, batch, emit, 0, unroll=True)


def _rglru(xr3, gr3, conv_w, conv_b, w_gates, b_r, b_i, lam, layer):
    b, s, _ = xr3.shape
    assert b == SUBLANES
    tc = LRU_TIME_CHUNK
    blk = (b, tc, LRU_WIDTH)
    vec = _layer_vec(LRU_WIDTH, layer)
    return pl.pallas_call(
        functools.partial(_rglru_kernel, batch=b),
        grid=(s // tc,),
        in_specs=[
            pl.BlockSpec(blk, lambda i: (0, i, 0)),
            pl.BlockSpec(blk, lambda i: (0, i, 0)),
            pl.BlockSpec((None, CONV_WIDTH, LRU_WIDTH), lambda i: (layer, 0, 0)),
            vec,
            pl.BlockSpec((None, 2, LRU_WIDTH // 2, LRU_WIDTH), lambda i: (layer, 0, 0, 0)),
            vec, vec, vec,
        ],
        out_specs=pl.BlockSpec(blk, lambda i: (0, i, 0)),
        out_shape=jax.ShapeDtypeStruct((b, s, LRU_WIDTH), bf16),
        scratch_shapes=[
            pltpu.VMEM((LRU_GROUPS, (tc + CONV_WIDTH - 1) * b, LANES), f32),
            pltpu.VMEM((LRU_GROUPS, tc * b, LANES), f32),
            pltpu.VMEM((LRU_GROUPS, tc * b, LANES), f32),
            pltpu.VMEM((LRU_GROUPS, b, LANES), f32),
        ],
        compiler_params=pltpu.CompilerParams(
            dimension_semantics=("arbitrary",), vmem_limit_bytes=VMEM_LIMIT),
        name="rglru",
    )(xr3, gr3, conv_w, conv_b, w_gates, b_r, b_i, lam)


def _gate_weights(w_r, w_i):
    per_half = N_LRU_BLOCKS // 2
    eye = jnp.eye(per_half, dtype=w_r.dtype)

    def dense(w):
        d = w.shape[0]
        blocks = w.reshape(d, 2, per_half, LRU_BLOCK, LRU_BLOCK)
        full = jnp.einsum('dhaij,ab->dhaibj', blocks, eye)
        return full.reshape(d, 2, per_half * LRU_BLOCK, per_half * LRU_BLOCK)

    return (jnp.concatenate([dense(w_r), dense(w_i)], axis=-1) * GATE_PRESCALE).astype(bf16)


def _out_mlp_kernel(*refs):
    attn_refs = refs[:HEAD_PAIRS]
    rec_ref, x_ref, gpost_ref, gpre2_ref, gpost2_ref, wo_ref, w1_ref, w2_ref, o_ref = refs[HEAD_PAIRS:]
    halves = [slice(i * ROW_TILE // 2, (i + 1) * ROW_TILE // 2) for i in range(2)]
    ff_chunk = 2 * N_CHUNK
    mix = [jnp.concatenate([a[r, :].astype(bf16) for a in attn_refs] + [rec_ref[r, :]], axis=-1)
           for r in halves]
    m = [jnp.dot(mx, wo_ref[...], preferred_element_type=f32) for mx in mix]
    x1, y = [], []
    for i, r in enumerate(halves):
        x1.append(x_ref[r, :] + _rms(m[i], gpost_ref[...]))
        h = _rms(x1[i], gpre2_ref[...]).astype(bf16)
        acc = None
        for n in range(D_FF // ff_chunk):
            cols = slice(n * ff_chunk, (n + 1) * ff_chunk)
            a = jnp.maximum(jnp.dot(h, w1_ref[:, cols], preferred_element_type=f32), 0.0)
            part = jnp.dot((a * a).astype(bf16), w2_ref[cols, :], preferred_element_type=f32)
            acc = part if acc is None else acc + part
        y.append(acc)
    for i, r in enumerate(halves):
        o_ref[r, :] = x1[i] + _rms(y[i], gpost2_ref[...])


def _out_mlp(attn, rec2d, x2d, g_post, g_pre2, g_post2, wo, w1, w2, layer):
    t = x2d.shape[0]
    tiles_per_seq = attn.shape[2] // ROW_TILE
    vec = _layer_vec(D_MODEL, layer)
    const = lambda shape: pl.BlockSpec(shape, lambda i: (0, 0), pipeline_mode=pl.Buffered(1))
    return pl.pallas_call(
        _out_mlp_kernel,
        grid=(t // ROW_TILE,),
        in_specs=[
            pl.BlockSpec((None, None, ROW_TILE, LANES),
                         lambda i, hp=hp: (i // tiles_per_seq, hp, i % tiles_per_seq, 0))
            for hp in range(HEAD_PAIRS)
        ] + [
            pl.BlockSpec((ROW_TILE, LRU_WIDTH), lambda i: (i, 0)),
            pl.BlockSpec((ROW_TILE, D_MODEL), lambda i: (i, 0)),
            vec, vec, vec,
            const((ATTN_WIDTH + LRU_WIDTH, D_MODEL)),
            const((D_MODEL, D_FF)),
            const((D_FF, D_MODEL)),
        ],
        out_specs=pl.BlockSpec((ROW_TILE, D_MODEL), lambda i: (i, 0)),
        out_shape=jax.ShapeDtypeStruct((t, D_MODEL), f32),
        compiler_params=pltpu.CompilerParams(
            dimension_semantics=("parallel",), vmem_limit_bytes=VMEM_LIMIT),
        name="out_mlp",
    )(*([attn] * HEAD_PAIRS), rec2d, x2d, g_post, g_pre2, g_post2, wo, w1, w2)


def kernel(x, mix_norm_pre, mix_norm_post, mlp_norm_pre, mlp_norm_post, w_in, conv_w, conv_b,
           w_rgate, b_rgate, w_igate, b_igate, lru_lambda, w_out, w_ff_in, w_ff_out):
    b, s, d = x.shape
    depth = w_in.shape[0]
    t = b * s
    x2d = x.reshape(t, d)
    rows = lambda v: v.reshape(depth, 1, -1)
    w_in_bf16 = w_in.astype(bf16)
    w_gates = _gate_weights(w_rgate, w_igate)
    b_r, b_i = rows(b_rgate) * GATE_PRESCALE, rows(b_igate) * GATE_PRESCALE
    for l in range(depth):
        qkv, xr, gr, wo, w1, w2 = _in_proj(x2d, rows(mix_norm_pre), w_in_bf16, w_out, w_ff_in,
                                           w_ff_out, l, b, s)
        attn = _attention(qkv)
        rec = _rglru(xr.reshape(b, s, LRU_WIDTH), gr.reshape(b, s, LRU_WIDTH), conv_w,
                     rows(conv_b), w_gates, b_r, b_i, rows(lru_lambda), l)
        x2d = _out_mlp(attn, rec.reshape(t, LRU_WIDTH), x2d,
                       rows(mix_norm_post), rows(mlp_norm_pre), rows(mlp_norm_post), wo, w1, w2, l)
    return x2d.reshape(b, s, d)
```
